```python
import math
import jax
import jax.numpy as jnp
from jax import lax
import numpy as np

D_MODEL = 1024
BATCH = 2
SEQ = 16384
DEPTH = 1
DEC_BATCH = 128
DEC_SEQ = 8
PAST_LEN = 8192
PAGE_SIZE = 128

HEAD_DIM = 64
MIX_WIDTH = D_MODEL
N_HEADS_A = MIX_WIDTH // 2 // HEAD_DIM
N_KV_A = N_HEADS_A // 2
N_HEADS_B = MIX_WIDTH // 2 // HEAD_DIM
N_KV_B = N_HEADS_B // 2
WIDTH_A = N_HEADS_A * HEAD_DIM
WIDTH_B = N_HEADS_B * HEAD_DIM
IDX_HEADS = 8
IDX_DIM = 64
TOPK_MAX = 256
MOBA_BLOCK = 256
MOBA_TOPK = 3
REL_BUCKETS = 32
REL_MAX_DIST = 128
QUERY_BLOCK = 128
EPS = 1e-6

kernel_name = 'hybrid_dsa_moba_decoder_step'


def _proj_sizes():
    return (N_HEADS_A * HEAD_DIM, N_KV_A * HEAD_DIM, N_KV_A * HEAD_DIM, WIDTH_A,
            IDX_HEADS * IDX_DIM, IDX_DIM, IDX_HEADS,
            N_HEADS_B * HEAD_DIM, N_KV_B * HEAD_DIM, N_KV_B * HEAD_DIM, WIDTH_B)


def _rmsnorm(x, g):
    xf = x.astype(jnp.float32)
    y = xf * lax.rsqrt(jnp.mean(xf * xf, axis=-1, keepdims=True) + EPS)
    return (y * g.astype(jnp.float32)).astype(x.dtype)


def _rel_bucket(dist):
    n = jnp.maximum(dist, 0)
    exact = REL_BUCKETS // 2
    nf = jnp.maximum(n, 1).astype(jnp.float32)
    large = exact + (jnp.log(nf / exact) / math.log(REL_MAX_DIST / exact) * (REL_BUCKETS - exact)).astype(jnp.int32)
    return jnp.where(n < exact, n, jnp.minimum(large, REL_BUCKETS - 1))


def _dense_rows(arr):
    last = arr.shape[1] - 1
    def get(b, s, g):
        return arr[b, jnp.clip(s, 0, last), g]
    return get


def _paged_rows(pool, page_table, fresh):
    page = pool.shape[1]
    past = page_table.shape[1] * page
    last_new = fresh.shape[1] - 1
    def get(b, s, g):
        sp = jnp.clip(s, 0, past - 1)
        old = pool[page_table[b, sp // page], sp % page, g]
        new = fresh[b, jnp.clip(s - past, 0, last_new), g]
        return jnp.where((s < past)[..., None], old, new)
    return get


def _to_chunks(a, qc):
    b, s = a.shape[:2]
    return jnp.moveaxis(a.reshape(b, s // qc, qc, *a.shape[2:]), 1, 0)


def _from_chunks(a):
    a = jnp.moveaxis(a, 0, 1)
    return a.reshape(a.shape[0], -1, *a.shape[3:])


def _branch_inputs(x, c, norm_g, w_ada, b_ada, w_in):
    lead = x.shape[:2]
    mod = jax.nn.silu(c) @ w_ada + b_ada
    shift, scale, gate = jnp.split(mod, 3, axis=-1)
    h = _rmsnorm(x, norm_g) * (1.0 + scale[:, None, :]) + shift[:, None, :]
    proj = h @ w_in
    offs, o = [], 0
    for sz in _proj_sizes()[:-1]:
        o += sz
        offs.append(o)
    qa, ka, va, za, qi, ki, wi, qb, kb, vb, zb = jnp.split(proj, offs, axis=-1)
    heads = lambda t, n: t.reshape(*lead, n, HEAD_DIM)
    return (gate, heads(qa, N_HEADS_A), heads(ka, N_KV_A), heads(va, N_KV_A), za,
            qi.reshape(*lead, IDX_HEADS, IDX_DIM), ki, wi * IDX_HEADS ** -0.5,
            heads(qb, N_HEADS_B), heads(kb, N_KV_B), heads(vb, N_KV_B), zb)


def _dsa_chunk(q, qi, wi, pos, idx_keys, get_k, get_v, bias_a, topk):
    bx, qc = q.shape[:2]
    grp = N_HEADS_A // N_KV_A
    n_keys = idx_keys.shape[1]
    dots = jnp.einsum('bqhd,bsd->bqhs', qi, idx_keys, preferred_element_type=jnp.float32) * IDX_DIM ** -0.5
    score = jnp.einsum('bqh,bqhs->bqs', wi.astype(jnp.float32), jax.nn.relu(dots))
    visible = jnp.arange(n_keys)[None, :] <= pos[:, None]
    score = jnp.where(visible[None], score, -jnp.inf)
    _, sel = lax.top_k(score, topk)
    bidx = jnp.arange(bx)[:, None, None, None]
    gidx = jnp.arange(N_KV_A)
    k = get_k(bidx, sel[..., None], gidx)
    v = get_v(bidx, sel[..., None], gidx)
    qg = q.reshape(bx, qc, N_KV_A, grp, HEAD_DIM)
    logits = jnp.einsum('bqgjd,bqkgd->bqgjk', qg, k, preferred_element_type=jnp.float32) * HEAD_DIM ** -0.5
    bias = bias_a[_rel_bucket(pos[None, :, None] - sel)].astype(jnp.float32)
    bias = bias.reshape(bx, qc, topk, N_KV_A, grp).transpose(0, 1, 3, 4, 2)
    valid = (sel <= pos[None, :, None])[:, :, None, None, :]
    p = jax.nn.softmax(jnp.where(valid, logits + bias, -jnp.inf), axis=-1)
    o = jnp.einsum('bqgjk,bqkgd->bqgjd', p.astype(v.dtype), v)
    return o.reshape(bx, qc, N_HEADS_A * HEAD_DIM)


def _moba_chunk(q, pos, blk_mean, get_k, get_v, bias_b, n_sel):
    bx, qc = q.shape[:2]
    grp = N_HEADS_B // N_KV_B
    scale = HEAD_DIM ** -0.5
    qg = q.reshape(bx, qc, N_KV_B, grp, HEAD_DIM)
    own_blk = pos[0] // MOBA_BLOCK
    s_own = own_blk * MOBA_BLOCK + jnp.arange(MOBA_BLOCK)
    bidx = jnp.arange(bx)
    gidx = jnp.arange(N_KV_B)
    k_own = get_k(bidx[:, None, None], s_own[None, :, None], gidx[None, None, :])
    v_own = get_v(bidx[:, None, None], s_own[None, :, None], gidx[None, None, :])
    lo = jnp.einsum('bqgjd,bkgd->bqgjk', qg, k_own, preferred_element_type=jnp.float32) * scale
    b_own = bias_b[_rel_bucket(pos[:, None] - s_own[None, :])].astype(jnp.float32)
    b_own = b_own.reshape(qc, MOBA_BLOCK, N_KV_B, grp).transpose(0, 2, 3, 1)
    ok_own = (s_own[None, :] <= pos[:, None])[None, :, None, None, :]
    lo = jnp.where(ok_own, lo + b_own[None], -jnp.inf)
    if n_sel == 0:
        p = jax.nn.softmax(lo, axis=-1)
        o = jnp.einsum('bqgjk,bkgd->bqgjd', p.astype(v_own.dtype), v_own)
        return o.reshape(bx, qc, N_HEADS_B * HEAD_DIM)
    gate = jnp.einsum('bqgjd,bngd->bqgjn', qg, blk_mean.astype(q.dtype), preferred_element_type=jnp.float32)
    past_ok = jnp.arange(blk_mean.shape[1]) < own_blk
    gate = jnp.where(past_ok, gate, -jnp.inf)
    _, sel = lax.top_k(gate, n_sel)
    n_rows = n_sel * MOBA_BLOCK
    s_sel = (sel[..., None] * MOBA_BLOCK + jnp.arange(MOBA_BLOCK)).reshape(bx, qc, N_KV_B, grp, n_rows)
    bsel = bidx[:, None, None, None, None]
    gsel = gidx[None, None, :, None, None]
    k_sel = get_k(bsel, s_sel, gsel)
    v_sel = get_v(bsel, s_sel, gsel)
    ls = jnp.einsum('bqgjd,bqgjkd->bqgjk', qg, k_sel, preferred_element_type=jnp.float32) * scale
    head = (gidx[:, None] * grp + jnp.arange(grp)[None, :])[None, None, :, :, None]
    b_sel = bias_b[_rel_bucket(pos[None, :, None, None, None] - s_sel), head].astype(jnp.float32)
    ok_sel = jnp.repeat(sel < own_blk, MOBA_BLOCK, axis=-1)
    ls = jnp.where(ok_sel, ls + b_sel, -jnp.inf)
    p = jax.nn.softmax(jnp.concatenate([ls, lo], axis=-1), axis=-1)
    o = (jnp.einsum('bqgjk,bqgjkd->bqgjd', p[..., :n_rows].astype(v_sel.dtype), v_sel)
         + jnp.einsum('bqgjk,bkgd->bqgjd', p[..., n_rows:].astype(v_own.dtype), v_own))
    return o.reshape(bx, qc, N_HEADS_B * HEAD_DIM)


def _mix(qa, qi, wi, qb, pos, idx_keys, get_ka, get_va, get_kb, get_vb, blk_mean, bias_a, bias_b, topk, n_sel, qc):
    def step(args):
        qa_c, qi_c, wi_c, qb_c, pos_c = args
        oa = _dsa_chunk(qa_c, qi_c, wi_c, pos_c, idx_keys, get_ka, get_va, bias_a, topk)
        ob = _moba_chunk(qb_c, pos_c, blk_mean, get_kb, get_vb, bias_b, n_sel)
        return oa, ob
    xs = (_to_chunks(qa, qc), _to_chunks(qi, qc), _to_chunks(wi, qc), _to_chunks(qb, qc), pos.reshape(-1, qc))
    oa, ob = lax.map(step, xs)
    return _from_chunks(oa), _from_chunks(ob)


def _residual(x, gate, oa, za, ob, zb, w_o):
    mixed = jnp.concatenate([oa * jax.nn.silu(za), ob * jax.nn.silu(zb)], axis=-1)
    return x + gate[:, None, :] * (mixed @ w_o)


def setup_inputs(seed: int = 0) -> dict:
    key = jax.random.key(seed)
    ks = jax.random.split(key, 17)
    f32 = jnp.float32
    n_pages = PAST_LEN // PAGE_SIZE
    n_used = DEC_BATCH * n_pages
    n_pool = n_used + max(1, n_used // 4)
    page_table = jax.random.permutation(ks[0], n_pool)[:n_used].reshape(DEC_BATCH, n_pages).astype(jnp.int32)
    proj_w = sum(_proj_sizes())
    mix_w = WIDTH_A + WIDTH_B
    nrm = lambda k, shape, s=1.0: s * jax.random.normal(k, shape, f32)
    return {
        'x_prompt': nrm(ks[1], (BATCH, SEQ, D_MODEL)),
        'x_sample': nrm(ks[2], (DEC_BATCH, DEC_SEQ, D_MODEL)),
        'c_prompt': nrm(ks[3], (BATCH, D_MODEL)),
        'c_sample': nrm(ks[4], (DEC_BATCH, D_MODEL)),
        'cache_k_a': nrm(ks[5], (DEPTH, n_pool, PAGE_SIZE, N_KV_A, HEAD_DIM)),
        'cache_v_a': nrm(ks[6], (DEPTH, n_pool, PAGE_SIZE, N_KV_A, HEAD_DIM)),
        'cache_idx_k': nrm(ks[7], (DEPTH, n_pool, PAGE_SIZE, IDX_DIM)),
        'cache_k_b': nrm(ks[8], (DEPTH, n_pool, PAGE_SIZE, N_KV_B, HEAD_DIM)),
        'cache_v_b': nrm(ks[9], (DEPTH, n_pool, PAGE_SIZE, N_KV_B, HEAD_DIM)),
        'page_table': page_table,
        'rel_bias': nrm(ks[10], (REL_BUCKETS, N_HEADS_A + N_HEADS_B), 0.5),
        'norm_g': 1.0 + nrm(ks[11], (DEPTH, D_MODEL), 0.02),
        'w_ada': nrm(ks[12], (DEPTH, D_MODEL, 3 * D_MODEL), 0.5 * D_MODEL ** -0.5),
        'b_ada': nrm(ks[13], (DEPTH, 3 * D_MODEL), 0.01),
        'w_in': nrm(ks[14], (DEPTH, D_MODEL, proj_w), D_MODEL ** -0.5),
        'w_o': nrm(ks[15], (DEPTH, mix_w, D_MODEL), mix_w ** -0.5),
        'final_g': 1.0 + nrm(ks[16], (D_MODEL,), 0.02),
    }


def reference(x_prompt, x_sample, c_prompt, c_sample, cache_k_a, cache_v_a, cache_idx_k, cache_k_b, cache_v_b,
              page_table, rel_bias, norm_g, w_ada, b_ada, w_in, w_o, final_g):
    f32 = jnp.float32
    bias_a = rel_bias[:, :N_HEADS_A]
    bias_b = rel_bias[:, N_HEADS_A:]
    bsz, seq = x_prompt.shape[:2]
    dbsz, dseq = x_sample.shape[:2]
    page = cache_k_a.shape[2]
    past = page_table.shape[1] * page
    pos_p = jnp.arange(seq, dtype=jnp.int32)
    pos_s = past + jnp.arange(dseq, dtype=jnp.int32)
    nb_p = -(-seq // MOBA_BLOCK)
    nsel_p = min(MOBA_TOPK, nb_p - 1)
    topk_p = min(TOPK_MAX, seq // 4)
    nfull_s = past // MOBA_BLOCK
    nsel_s = min(MOBA_TOPK, nfull_s)
    topk_s = min(TOPK_MAX, (past + dseq) // 4)
    pages_per_blk = MOBA_BLOCK // page

    hp, hs = x_prompt, x_sample
    kap, vap, ikp, kbp, vbp = [], [], [], [], []
    kas, vas, iks, kbs, vbs = [], [], [], [], []
    for l in range(DEPTH):
        gp, qa, ka, va, za, qi, ki, wi, qb, kb, vb, zb = _branch_inputs(hp, c_prompt, norm_g[l], w_ada[l], b_ada[l], w_in[l])
        kb_pad = jnp.pad(kb, ((0, 0), (0, nb_p * MOBA_BLOCK - seq), (0, 0), (0, 0)))
        mean_p = jnp.mean(kb_pad.astype(f32).reshape(bsz, nb_p, MOBA_BLOCK, N_KV_B, HEAD_DIM), axis=2)
        oa, ob = _mix(qa, qi, wi, qb, pos_p, ki, _dense_rows(ka), _dense_rows(va), _dense_rows(kb), _dense_rows(vb),
                      mean_p, bias_a, bias_b, topk_p, nsel_p, QUERY_BLOCK)
        hp = _residual(hp, gp, oa, za, ob, zb, w_o[l])

        gs, qa_s, ka_s, va_s, za_s, qi_s, ki_s, wi_s, qb_s, kb_s, vb_s, zb_s = _branch_inputs(
            hs, c_sample, norm_g[l], w_ada[l], b_ada[l], w_in[l])
        idx_past = cache_idx_k[l][page_table].reshape(dbsz, past, IDX_DIM)
        idx_all = jnp.concatenate([idx_past, ki_s.astype(idx_past.dtype)], axis=1)
        page_mean = jnp.mean(cache_k_b[l].astype(f32), axis=1)
        mean_s = jnp.mean(page_mean[page_table[:, :nfull_s * pages_per_blk]].reshape(
            dbsz, nfull_s, pages_per_blk, N_KV_B, HEAD_DIM), axis=2)
        oa_s, ob_s = _mix(qa_s, qi_s, wi_s, qb_s, pos_s, idx_all,
                          _paged_rows(cache_k_a[l], page_table, ka_s), _paged_rows(cache_v_a[l], page_table, va_s),
                          _paged_rows(cache_k_b[l], page_table, kb_s), _paged_rows(cache_v_b[l], page_table, vb_s),
                          mean_s, bias_a, bias_b, topk_s, nsel_s, 1)
        hs = _residual(hs, gs, oa_s, za_s, ob_s, zb_s, w_o[l])

        kap.append(ka); vap.append(va); ikp.append(ki); kbp.append(kb); vbp.append(vb)
        kas.append(ka_s); vas.append(va_s); iks.append(ki_s); kbs.append(kb_s); vbs.append(vb_s)

    y_prompt = _rmsnorm(hp, final_g)
    y_sample = _rmsnorm(hs, final_g)
    return (y_prompt, y_sample,
            jnp.stack(kap), jnp.stack(vap), jnp.stack(ikp), jnp.stack(kbp), jnp.stack(vbp),
            jnp.stack(kas), jnp.stack(vas), jnp.stack(iks), jnp.stack(kbs), jnp.stack(vbs))
```

```python
import functools
import math

import numpy as np
import jax
import jax.numpy as jnp
from jax import lax
from jax.experimental import pallas as pl
from jax.experimental.pallas import tpu as pltpu

F32 = jnp.float32
BF16 = jnp.bfloat16
I32 = jnp.int32

HEAD_DIM = 64
N_HEADS = 8
N_KV = 4
GRP = N_HEADS // N_KV
WIDTH = N_HEADS * HEAD_DIM
KVW = N_KV * HEAD_DIM
IDX_HEADS = 8
IDX_DIM = 64
TOPK_MAX = 256
MOBA_BLOCK = 256
MOBA_TOPK = 3
REL_BUCKETS = 32
REL_MAX_DIST = 128
EPS = 1e-6
NEG = -1e30
LANES = 128
VMEM_LIMIT = 56 * 1024 * 1024

_ORIG = (("qa", WIDTH), ("ka", KVW), ("va", KVW), ("za", WIDTH), ("qi", IDX_HEADS * IDX_DIM), ("ki", IDX_DIM),
         ("wi", IDX_HEADS), ("qb", WIDTH), ("kb", KVW), ("vb", KVW), ("zb", WIDTH))
_COLS_SAMPLE = ("qa", "ka", "va", "za", "qi", "qb", "kb", "vb", "zb", "ki", "wi")
_COLS_PROMPT = ("qa", "za", "qi", "qb", "zb", "va", "vb", "wi")
_ROWS_PROMPT_T = ("ka", "va", "kb", "vb", "ki")


def _col_ranges():
    off, o = {}, 0
    for name, sz in _ORIG:
        off[name] = (o, o + sz)
        o += sz
    return off


def _offsets(order):
    sizes = dict(_ORIG)
    off, o = {}, 0
    for name in order:
        off[name] = (o, o + sizes[name])
        o += sizes[name]
    return off, o


def _pick_cols(w, order):
    cols = _col_ranges()
    parts = [w[:, cols[n][0]:cols[n][1]] for n in order]
    pad = (-sum(p.shape[1] for p in parts)) % LANES
    if pad:
        parts.append(jnp.zeros((w.shape[0], pad), w.dtype))
    return jnp.concatenate(parts, axis=1)


def _bucket_np(n):
    n = np.maximum(np.asarray(n, np.int64), 0)
    exact = REL_BUCKETS // 2

    def large(dt):
        nf = np.maximum(n, 1).astype(dt)
        return exact + (np.log(nf / dt(exact)) / dt(math.log(REL_MAX_DIST / exact)) * dt(REL_BUCKETS - exact)).astype(np.int64)

    l32, l64 = large(np.float32), large(np.float64)
    assert np.array_equal(np.minimum(l32, REL_BUCKETS - 1)[n >= exact], np.minimum(l64, REL_BUCKETS - 1)[n >= exact])
    return np.where(n < exact, n, np.minimum(l32, REL_BUCKETS - 1)).astype(np.int32)


def _bias_table(bias, dist, valid):
    t = jnp.transpose(bias.astype(F32)[_bucket_np(dist)], (2, 0, 1))
    return jnp.where(jnp.asarray(valid)[None], t, -jnp.inf)


def _stack_heads(t, rows):
    return t.reshape(N_KV, GRP * rows, t.shape[-1])


def _mod_kernel(c_ref, w_ref, b_ref, o_ref):
    c = c_ref[...]
    s = c * jax.nn.sigmoid(c)
    o_ref[...] = jnp.dot(s, w_ref[...], preferred_element_type=F32, precision=lax.Precision.HIGHEST) + b_ref[...]


def _mod_call(c, w_ada, b_ada):
    n, d = c.shape
    d3 = w_ada.shape[1]
    tn = 512 if d3 % 512 == 0 else d3
    return pl.pallas_call(
        _mod_kernel,
        grid=(d3 // tn,),
        in_specs=[pl.BlockSpec((n, d), lambda j: (0, 0)),
                  pl.BlockSpec((d, tn), lambda j: (0, j)),
                  pl.BlockSpec((1, tn), lambda j: (0, j))],
        out_specs=pl.BlockSpec((n, tn), lambda j: (0, j)),
        out_shape=jax.ShapeDtypeStruct((n, d3), F32),
        compiler_params=pltpu.CompilerParams(dimension_semantics=("arbitrary",), vmem_limit_bytes=VMEM_LIMIT),
        name="adaln_mod",
    )(c, w_ada, b_ada.reshape(1, d3))


def _proj_kernel(*refs, prompt, tm):
    if prompt:
        (x_ref, sc_ref, sh_ref, g_ref, w_ref, wt_ref,
         qa_o, qi_o, qb_o, za_o, zb_o, wi_o, vab_o, vbb_o,
         kat_o, vat_o, kbt_o, vbt_o, kit_o, katb_o, kbtb_o, kitb_o, mean_o) = refs
        off, _ = _offsets(_COLS_PROMPT)
    else:
        (x_ref, sc_ref, sh_ref, g_ref, w_ref,
         qa_o, qi_o, qb_o, za_o, zb_o, wi_o, ka_o, va_o, kb_o, vb_o, ki_o) = refs
        off, _ = _offsets(_COLS_SAMPLE)
    x = x_ref[...]
    y = x * lax.rsqrt(jnp.mean(x * x, axis=-1, keepdims=True) + EPS) * g_ref[...]
    h = y * (1.0 + sc_ref[...]) + sh_ref[...]
    hb = h.astype(BF16)

    def seg(name):
        a, b = off[name]
        return jnp.dot(hb, w_ref[:, a:b], preferred_element_type=F32)

    qscale = HEAD_DIM ** -0.5
    qa_o[...] = (seg("qa") * qscale).astype(BF16)
    qi_o[...] = (seg("qi") * (IDX_DIM ** -0.5)).astype(BF16)
    qb_o[...] = (seg("qb") * qscale).astype(BF16)
    za_o[...] = seg("za")
    zb_o[...] = seg("zb")
    if prompt:
        a, _ = off["wi"]
        wi_o[...] = jnp.dot(hb, w_ref[:, a:a + LANES], preferred_element_type=F32)[:, :IDX_HEADS] * (IDX_HEADS ** -0.5)
        vab_o[...] = seg("va").astype(BF16)
        vbb_o[...] = seg("vb").astype(BF16)
        roff, _ = _offsets(_ROWS_PROMPT_T)
        nt = (((1,), (1,)), ((), ()))
        kt = lax.dot_general(wt_ref[...], hb, nt, preferred_element_type=F32)
        part = lambda name: kt[roff[name][0]:roff[name][1]]
        kat_o[...] = part("ka")
        vat_o[...] = part("va")
        kbt_o[...] = part("kb")
        vbt_o[...] = part("vb")
        kit_o[...] = part("ki")
        katb_o[...] = part("ka").astype(BF16)
        kbtb_o[...] = part("kb").astype(BF16)
        kitb_o[...] = part("ki").astype(BF16)
        kbt = part("kb")
        for r in range(tm // MOBA_BLOCK):
            mean_o[:, r:r + 1] = jnp.mean(kbt[:, r * MOBA_BLOCK:(r + 1) * MOBA_BLOCK], axis=1, keepdims=True)
    else:
        ka_o[...] = seg("ka")
        va_o[...] = seg("va")
        kb_o[...] = seg("kb")
        vb_o[...] = seg("vb")
        a, _ = off["ki"]
        kiwi = jnp.dot(hb, w_ref[:, a:a + LANES], preferred_element_type=F32)
        ki_o[...] = kiwi[:, :IDX_DIM]
        wi_o[...] = kiwi[:, IDX_DIM:IDX_DIM + IDX_HEADS] * (IDX_HEADS ** -0.5)


def _proj_call(x, scale, shift, g, w_cols, w_rows_t, *, per_row_mod):
    bx, s, d = x.shape
    tm = min(512, s)
    assert s % tm == 0 and tm % 8 == 0
    prompt = w_rows_t is not None
    nw = w_cols.shape[1]
    row = lambda c: pl.BlockSpec((None, tm, c), lambda b, i: (b, i, 0))
    mod_spec = row(d) if per_row_mod else pl.BlockSpec((None, 1, d), lambda b, i: (b, 0, 0))
    in_specs = [row(d), mod_spec, mod_spec,
                pl.BlockSpec((1, d), lambda b, i: (0, 0)),
                pl.BlockSpec((d, nw), lambda b, i: (0, 0))]
    args = [x, scale, shift, g.reshape(1, d), w_cols]
    sds = lambda c, dt: jax.ShapeDtypeStruct((bx, s, c), dt)
    out_shape = [sds(WIDTH, BF16), sds(WIDTH, BF16), sds(WIDTH, BF16), sds(WIDTH, F32), sds(WIDTH, F32),
                 sds(IDX_HEADS, F32)]
    out_specs = [row(WIDTH)] * 5 + [row(IDX_HEADS)]
    if prompt:
        assert tm % MOBA_BLOCK == 0
        in_specs.append(pl.BlockSpec(w_rows_t.shape, lambda b, i: (0, 0)))
        args.append(w_rows_t)
        col_t = lambda r: pl.BlockSpec((None, r, tm), lambda b, i: (b, 0, i))
        sds_t = lambda r, dt: jax.ShapeDtypeStruct((bx, r, s), dt)
        out_shape += [sds(KVW, BF16), sds(KVW, BF16),
                      sds_t(KVW, F32), sds_t(KVW, F32), sds_t(KVW, F32), sds_t(KVW, F32), sds_t(IDX_DIM, F32),
                      sds_t(KVW, BF16), sds_t(KVW, BF16), sds_t(IDX_DIM, BF16),
                      jax.ShapeDtypeStruct((bx, s // tm, KVW, tm // MOBA_BLOCK), F32)]
        out_specs += [row(KVW), row(KVW),
                      col_t(KVW), col_t(KVW), col_t(KVW), col_t(KVW), col_t(IDX_DIM),
                      col_t(KVW), col_t(KVW), col_t(IDX_DIM),
                      pl.BlockSpec((None, None, KVW, tm // MOBA_BLOCK), lambda b, i: (b, i, 0, 0))]
    else:
        out_shape += [sds(KVW, F32), sds(KVW, F32), sds(KVW, F32), sds(KVW, F32), sds(IDX_DIM, F32)]
        out_specs += [row(KVW)] * 4 + [row(IDX_DIM)]
    return pl.pallas_call(
        functools.partial(_proj_kernel, prompt=prompt, tm=tm),
        grid=(bx, s // tm),
        in_specs=in_specs, out_specs=out_specs, out_shape=out_shape,
        compiler_params=pltpu.CompilerParams(dimension_semantics=("arbitrary", "arbitrary"),
                                             vmem_limit_bytes=VMEM_LIMIT),
        name="norm_mod_proj",
    )(*args)


def _flash_init(m_scr, l_scr, acc_scr):
    m_scr[...] = jnp.full(m_scr.shape, NEG, F32)
    l_scr[...] = jnp.zeros(l_scr.shape, F32)
    acc_scr[...] = jnp.zeros(acc_scr.shape, F32)


def _flash_update(g, logits, shift, v_blk, m_scr, l_scr, acc_scr):
    m_old = m_scr[g]
    mb = jnp.max(logits, axis=1, keepdims=True)
    if shift is not None:
        mb = mb + shift
    m_new = jnp.maximum(m_old, mb)
    alpha = jnp.exp(m_old - m_new)
    sub = m_new if shift is None else m_new - shift
    p = jnp.exp(logits - sub)
    l_scr[g] = alpha * l_scr[g] + jnp.sum(p, axis=1, keepdims=True)
    acc_scr[g] = alpha * acc_scr[g] + jnp.dot(p.astype(BF16), v_blk, preferred_element_type=F32)
    m_scr[g] = m_new


def _flash_finish(o_ref, l_scr, acc_scr, tq):
    for g in range(N_KV):
        o = acc_scr[g] / l_scr[g]
        c0 = (g % 2) * HEAD_DIM
        for j in range(GRP):
            hd = g * GRP + j
            o_ref[:, hd * HEAD_DIM:(hd + 1) * HEAD_DIM] = o[j * tq:(j + 1) * tq, c0:c0 + HEAD_DIM]


def _f2key(x):
    b = lax.bitcast_convert_type(x, I32)
    return b ^ ((b >> 31) & jnp.int32(0x7FFFFFFF))


def _key2f(k):
    return lax.bitcast_convert_type(k ^ ((k >> 31) & jnp.int32(0x7FFFFFFF)), F32)


def _kth_threshold(count_gt, rowmax, small, k):
    kf = jnp.float32(k)
    lo0 = jnp.full(rowmax.shape, _f2key(jnp.float32(-jnp.inf)), I32)
    hi0 = jnp.where(small, lo0, _f2key(rowmax))
    st0 = (jnp.int32(0), lo0, hi0, jnp.zeros(rowmax.shape, F32), small.astype(I32))

    def cond(st):
        return jnp.logical_and(st[0] < 34, jnp.min(st[4]) == 0)

    def body(st):
        it, lo, hi, chi, done = st
        mid = (lo & hi) + ((lo ^ hi) >> 1)
        conv = mid == lo
        cnt = count_gt(_key2f(mid))
        active = jnp.logical_and(done == 0, jnp.logical_not(conv))
        up = jnp.logical_and(active, cnt > kf)
        down = jnp.logical_and(active, cnt <= kf)
        lo = jnp.where(up, mid, lo)
        hi = jnp.where(down, mid, hi)
        chi = jnp.where(down, cnt, chi)
        fin = jnp.logical_or(conv, jnp.logical_and(active, cnt == kf))
        return it + 1, lo, hi, chi, jnp.where(fin, 1, done)

    _, _, hi, chi, _ = lax.while_loop(cond, body, st0)
    return _key2f(hi), chi


def _dsa_prompt_kernel(qi_ref, wi_ref, qa_ref, kit_ref, kat_ref, va_ref, tab_ref, b31_ref, o_ref,
                       s_scr, qis_scr, wir_scr, qst_scr, m_scr, l_scr, acc_scr, *, tq, fw, topk):
    i = pl.program_id(1)
    q0 = i * tq
    npc = fw // LANES

    qi = qi_ref[...]
    wi = wi_ref[...]
    qa = qa_ref[...]
    for h in range(IDX_HEADS):
        qis_scr[h * tq:(h + 1) * tq, :] = qi[:, h * IDX_DIM:(h + 1) * IDX_DIM]
        wir_scr[h] = jnp.broadcast_to(wi[:, h:h + 1], (tq, fw))
    for g in range(N_KV):
        for j in range(GRP):
            hd = g * GRP + j
            qst_scr[g, j * tq:(j + 1) * tq, :] = qa[:, hd * HEAD_DIM:(hd + 1) * HEAD_DIM]

    rowid = q0 + lax.broadcasted_iota(I32, (tq, 1), 0)

    nb1 = (q0 + tq + fw - 1) // fw

    def p1(jb, rmax):
        k0 = pl.multiple_of(jb * fw, fw)
        d = jnp.dot(qis_scr[...], kit_ref[:, pl.ds(k0, fw)], preferred_element_type=F32)
        acc = jnp.maximum(d[0:tq], 0.0) * wir_scr[0]
        for h in range(1, IDX_HEADS):
            acc = acc + jnp.maximum(d[h * tq:(h + 1) * tq], 0.0) * wir_scr[h]
        col = k0 + lax.broadcasted_iota(I32, (tq, fw), 1)
        sc = jnp.where(col <= rowid, acc, -jnp.inf)
        s_scr[:, pl.ds(k0, fw)] = sc
        part = sc[:, 0:LANES]
        for c in range(1, npc):
            part = jnp.maximum(part, sc[:, c * LANES:(c + 1) * LANES])
        return jnp.maximum(rmax, part)

    rmax = lax.fori_loop(0, nb1, p1, jnp.full((tq, LANES), -jnp.inf, F32))
    rowmax = jnp.max(rmax, axis=1, keepdims=True)

    def count_gt(t):
        def cb(jb, c):
            x = s_scr[:, pl.ds(pl.multiple_of(jb * fw, fw), fw)]
            one = jnp.where(x > t, 1.0, 0.0)
            part = one[:, 0:LANES]
            for cc in range(1, npc):
                part = part + one[:, cc * LANES:(cc + 1) * LANES]
            return c + part
        c = lax.fori_loop(0, nb1, cb, jnp.zeros((tq, LANES), F32))
        return jnp.sum(c, axis=1, keepdims=True)

    small = rowid < topk
    hi, chi = _kth_threshold(count_gt, rowmax, small, topk)
    need = jnp.where(small, 0.0, jnp.float32(topk) - chi)
    any_tie = jnp.max(need) > 0.0

    _flash_init(m_scr, l_scr, acc_scr)
    nfar = jnp.maximum(q0 - LANES, 0) // fw
    near0 = nfar * npc

    def select(x, eqb, with_ties, width):
        gt = x > hi
        if not with_ties:
            return gt, eqb
        eq = x == hi
        e = jnp.where(eq, 1.0, 0.0)
        r = lax.broadcasted_iota(I32, (width, width), 0)
        c = lax.broadcasted_iota(I32, (width, width), 1)
        upper = jnp.where(r < c, 1.0, 0.0).astype(BF16)
        rank = jnp.dot(e.astype(BF16), upper, preferred_element_type=F32) + eqb
        sel = jnp.logical_or(gt, jnp.logical_and(eq, rank < need))
        return sel, eqb + jnp.sum(e, axis=1, keepdims=True)

    def attend(with_ties):
        def far(jb, eqb):
            k0 = pl.multiple_of(jb * fw, fw)
            sel, eqb = select(s_scr[:, pl.ds(k0, fw)], eqb, with_ties, fw)
            madd = jnp.where(sel, 0.0, -jnp.inf)
            for g in range(N_KV):
                lg = jnp.dot(qst_scr[g], kat_ref[g * HEAD_DIM:(g + 1) * HEAD_DIM, pl.ds(k0, fw)],
                             preferred_element_type=F32)
                lg = (lg.reshape(GRP, tq, fw) + madd[None]).reshape(GRP * tq, fw)
                p0 = (g // 2) * LANES
                _flash_update(g, lg, b31_ref[g], va_ref[pl.ds(k0, fw), p0:p0 + LANES], m_scr, l_scr, acc_scr)
            return eqb

        def near(jj, eqb):
            k0 = pl.multiple_of(jj * LANES, LANES)
            t = jnp.clip(jj - i + 2, 0, 2)
            sel, eqb = select(s_scr[:, pl.ds(k0, LANES)], eqb, with_ties, LANES)
            madd = jnp.where(sel, 0.0, -jnp.inf)
            for g in range(N_KV):
                lg = jnp.dot(qst_scr[g], kat_ref[g * HEAD_DIM:(g + 1) * HEAD_DIM, pl.ds(k0, LANES)],
                             preferred_element_type=F32)
                lg = (lg.reshape(GRP, tq, LANES) + madd[None]).reshape(GRP * tq, LANES) + tab_ref[g, t]
                p0 = (g // 2) * LANES
                _flash_update(g, lg, None, va_ref[pl.ds(k0, LANES), p0:p0 + LANES], m_scr, l_scr, acc_scr)
            return eqb

        eqb = lax.fori_loop(0, nfar, far, jnp.zeros((tq, 1), F32))
        lax.fori_loop(near0, i + 1, near, eqb)

    @pl.when(any_tie)
    def _():
        attend(True)

    @pl.when(jnp.logical_not(any_tie))
    def _():
        attend(False)

    _flash_finish(o_ref, l_scr, acc_scr, tq)


def _dsa_prompt(qi, wi, qa, kit, kat, va_bf, bias_a, topk):
    bx, s, _ = qa.shape
    tq = LANES
    fw = min(512, s)
    assert s % fw == 0 and s % tq == 0 and REL_MAX_DIST <= LANES
    r = np.arange(tq)[:, None]
    c = np.arange(LANES)[None, :]
    slabs = []
    for dist, valid in ((np.full((tq, LANES), REL_MAX_DIST), np.ones((tq, LANES), bool)),
                        (r - c + LANES, np.ones((tq, LANES), bool)),
                        (r - c, (r - c) >= 0)):
        slabs.append(_stack_heads(_bias_table(bias_a, dist, valid), tq))
    tab = jnp.stack(slabs, axis=1)
    b31 = jnp.repeat(bias_a[REL_BUCKETS - 1].astype(F32), tq).reshape(N_KV, GRP * tq, 1)
    res = lambda shape: pl.BlockSpec((None,) + shape, lambda b, i: (b, 0, 0), pipeline_mode=pl.Buffered(1))
    const = lambda a: pl.BlockSpec(a.shape, lambda b, i: (0,) * a.ndim, pipeline_mode=pl.Buffered(1))
    blk = lambda w: pl.BlockSpec((None, tq, w), lambda b, i: (b, i, 0))
    return pl.pallas_call(
        functools.partial(_dsa_prompt_kernel, tq=tq, fw=fw, topk=topk),
        grid=(bx, s // tq),
        in_specs=[blk(IDX_HEADS * IDX_DIM), blk(IDX_HEADS), blk(WIDTH),
                  res((IDX_DIM, s)), res((KVW, s)), res((s, KVW)), const(tab), const(b31)],
        out_specs=blk(WIDTH),
        out_shape=jax.ShapeDtypeStruct((bx, s, WIDTH), F32),
        scratch_shapes=[pltpu.VMEM((tq, s), F32),
                        pltpu.VMEM((IDX_HEADS * tq, IDX_DIM), BF16),
                        pltpu.VMEM((IDX_HEADS, tq, fw), F32),
                        pltpu.VMEM((N_KV, GRP * tq, HEAD_DIM), BF16),
                        pltpu.VMEM((N_KV, GRP * tq, 1), F32),
                        pltpu.VMEM((N_KV, GRP * tq, 1), F32),
                        pltpu.VMEM((N_KV, GRP * tq, LANES), F32)],
        compiler_params=pltpu.CompilerParams(dimension_semantics=("arbitrary", "arbitrary"),
                                             vmem_limit_bytes=VMEM_LIMIT),
        name="dsa_prompt",
    )(qi, wi, qa, kit, kat, va_bf, tab, b31)


def _top_blocks(gate, blk, n_valid, n_sel):
    nb = gate.shape[1]
    gate = jnp.where(blk < n_valid, gate, -jnp.inf)
    sel = jnp.zeros(gate.shape, jnp.bool_)
    for _ in range(n_sel):
        mx = jnp.max(gate, axis=1, keepdims=True)
        first = jnp.min(jnp.where(gate == mx, blk, nb), axis=1, keepdims=True)
        pick = blk == first
        sel = jnp.logical_or(sel, pick)
        gate = jnp.where(pick, -jnp.inf, gate)
    return jnp.logical_and(sel, blk < n_valid)


def _moba_prompt_kernel(qb_ref, kbt_ref, vb_ref, mean_ref, oh_ref, tab_ref, b31_ref, o_ref,
                        qext_scr, m_scr, l_scr, acc_scr, *, tq, nb, nbp, n_sel):
    i = pl.program_id(1)
    q0 = i * tq
    own = q0 // MOBA_BLOCK
    par = (q0 % MOBA_BLOCK) // tq

    qb = qb_ref[...]
    blk = lax.broadcasted_iota(I32, (GRP * tq, nbp), 1)
    for g in range(N_KV):
        qg = jnp.concatenate([qb[:, (g * GRP + j) * HEAD_DIM:(g * GRP + j + 1) * HEAD_DIM] for j in range(GRP)], axis=0)
        mg = mean_ref[g * HEAD_DIM:(g + 1) * HEAD_DIM, :].astype(BF16)
        gate = jnp.dot(qg, mg, preferred_element_type=F32)
        ok = jnp.logical_or(_top_blocks(gate, blk, own, n_sel), blk == own)
        qext_scr[g, :, 0:HEAD_DIM] = qg
        qext_scr[g, :, HEAD_DIM:HEAD_DIM + nbp] = jnp.where(ok, 0.0, NEG).astype(BF16)

    _flash_init(m_scr, l_scr, acc_scr)

    def block(n, kind):
        c0 = pl.multiple_of(n * MOBA_BLOCK, MOBA_BLOCK)
        ohb = oh_ref[:, pl.ds(c0, MOBA_BLOCK)]
        for g in range(N_KV):
            kext = jnp.concatenate([kbt_ref[g * HEAD_DIM:(g + 1) * HEAD_DIM, pl.ds(c0, MOBA_BLOCK)], ohb], axis=0)
            lg = jnp.dot(qext_scr[g], kext, preferred_element_type=F32)
            p0 = (g // 2) * LANES
            v = vb_ref[pl.ds(c0, MOBA_BLOCK), p0:p0 + LANES]
            if kind == 0:
                _flash_update(g, lg, b31_ref[g], v, m_scr, l_scr, acc_scr)
            else:
                _flash_update(g, lg + tab_ref[g, par, kind - 1], None, v, m_scr, l_scr, acc_scr)

    def far(n, carry):
        block(n, 0)
        return carry

    lax.fori_loop(0, jnp.maximum(own - 1, 0), far, 0)

    @pl.when(own >= 1)
    def _():
        block(own - 1, 1)

    block(own, 2)
    _flash_finish(o_ref, l_scr, acc_scr, tq)


def _moba_prompt(qb, kbt, vb_bf, means, bias_b):
    bx, s, _ = qb.shape
    tq = LANES
    nb = s // MOBA_BLOCK
    nbp = HEAD_DIM
    assert s % MOBA_BLOCK == 0 and nb <= nbp and MOBA_BLOCK % tq == 0 and REL_MAX_DIST <= MOBA_BLOCK
    n_sel = min(MOBA_TOPK, nb - 1)
    means = jnp.pad(means, ((0, 0), (0, 0), (0, nbp - nb)))
    oh =jnp.asarray((np.arange(s)[None, :] // MOBA_BLOCK == np.arange(nbp)[:, None]).astype(np.float32), BF16)
    r = np.arange(tq)[:, None]
    c = np.arange(MOBA_BLOCK)[None, :]
    tabs = []
    for par in range(MOBA_BLOCK // tq):
        d_prev = par * tq + r - c + MOBA_BLOCK
        d_own = par * tq + r - c
        tabs.append(jnp.stack([_stack_heads(_bias_table(bias_b, d_prev, np.ones_like(d_prev, bool)), tq),
                               _stack_heads(_bias_table(bias_b, d_own, d_own >= 0), tq)], axis=1))
    tab = jnp.stack(tabs, axis=1)
    b31 = jnp.repeat(bias_b[REL_BUCKETS - 1].astype(F32), tq).reshape(N_KV, GRP * tq, 1)
    res = lambda shape: pl.BlockSpec((None,) + shape, lambda b, i: (b, 0, 0), pipeline_mode=pl.Buffered(1))
    const = lambda a: pl.BlockSpec(a.shape, lambda b, i: (0,) * a.ndim, pipeline_mode=pl.Buffered(1))
    blk = lambda w: pl.BlockSpec((None, tq, w), lambda b, i: (b, i, 0))
    return pl.pallas_call(
        functools.partial(_moba_prompt_kernel, tq=tq, nb=nb, nbp=nbp, n_sel=n_sel),
        grid=(bx, s // tq),
        in_specs=[blk(WIDTH), res((KVW, s)), res((s, KVW)), res((KVW, nbp)), const(oh), const(tab), const(b31)],
        out_specs=blk(WIDTH),
        out_shape=jax.ShapeDtypeStruct((bx, s, WIDTH), F32),
        scratch_shapes=[pltpu.VMEM((N_KV, GRP * tq, HEAD_DIM + nbp), BF16),
                        pltpu.VMEM((N_KV, GRP * tq, 1), F32),
                        pltpu.VMEM((N_KV, GRP * tq, 1), F32),
                        pltpu.VMEM((N_KV, GRP * tq, LANES), F32)],
        compiler_params=pltpu.CompilerParams(dimension_semantics=("arbitrary", "arbitrary"),
                                             vmem_limit_bytes=VMEM_LIMIT),
        name="moba_prompt",
    )(qb, kbt, vb_bf, means, oh, tab, b31)


def _sample_kernel(*refs, mode, past, page, ds, ck, topk, n_sel):
    if mode == "dsa":
        (pt_ref, qbd_ref, kf_ref, vf_ref, btab_ref, b31_ref, qis_ref, wic_ref, kif_ref,
         ipool, kpool, vpool, o_ref, kbuf, vbuf, ibuf, sem, lg_scr, madd_scr, sc_scr) = refs
        pools, bufs = (kpool, vpool, ipool), (kbuf, vbuf, ibuf)
    else:
        (pt_ref, qbd_ref, kf_ref, vf_ref, btab_ref, b31_ref,
         kpool, vpool, o_ref, kbuf, vbuf, sem, lg_scr, madd_scr, mean_scr) = refs
        pools, bufs = (kpool, vpool), (kbuf, vbuf)
    b = pl.program_id(0)
    nseq = pl.num_programs(0)
    slot = b % 2
    npages = past // page
    nck = past // ck
    lk = past + LANES
    rows = N_HEADS * ds
    nt = (((1,), (1,)), ((), ()))

    def copies(seq, sl):
        out = []
        for a, (pool, buf) in enumerate(zip(pools, bufs)):
            def mk(pg, pool=pool, buf=buf, a=a):
                return pltpu.make_async_copy(pool.at[pt_ref[seq, pg]],
                                             buf.at[sl, :, pl.ds(pl.multiple_of(pg * page, page), page)],
                                             sem.at[a, sl])
            out.append(mk)
        return out

    def start_all(seq, sl):
        def body(pg, c):
            for mk in copies(seq, sl):
                mk(pg).start()
            return c
        lax.fori_loop(0, npages, body, 0)

    def wait_all(seq, sl):
        def body(pg, c):
            for mk in copies(seq, sl):
                mk(pg).wait()
            return c
        lax.fori_loop(0, npages, body, 0)

    @pl.when(b == 0)
    def _():
        start_all(b, slot)

    @pl.when(b + 1 < nseq)
    def _():
        start_all(b + 1, 1 - slot)

    wait_all(b, slot)

    def tail(ref):
        return jnp.concatenate([ref[...], jnp.zeros((LANES - ds, ref.shape[1]), F32)], axis=0).astype(BF16)

    kf_tail = tail(kf_ref)
    vf_tail = tail(vf_ref)
    qrow = lax.broadcasted_iota(I32, (ds, 1), 0)

    if mode == "dsa":
        qis = qis_ref[...]
        wic = wic_ref[...]

        def head_sum(d):
            t = jnp.maximum(d, 0.0) * wic
            return jnp.sum(t.reshape(IDX_HEADS, ds, d.shape[1]), axis=0)

        def sc_body(c, carry):
            r0 = pl.multiple_of(c * ck, ck)
            d = jnp.dot(qis, ibuf[slot, :, pl.ds(r0, ck)].astype(BF16), preferred_element_type=F32)
            sc_scr[:, pl.ds(r0, ck)] = head_sum(d)
            return carry

        lax.fori_loop(0, nck, sc_body, 0)
        lane = lax.broadcasted_iota(I32, (ds, LANES), 1)
        d_tail = lax.dot_general(qis, tail(kif_ref), nt, preferred_element_type=F32)
        sc_scr[:, past:lk] = jnp.where(lane <= qrow, head_sum(d_tail), -jnp.inf)

        x = sc_scr[...]
        rowmax = jnp.max(x, axis=1, keepdims=True)
        small = (past + qrow) < topk

        def count_gt(t):
            return jnp.sum(jnp.where(sc_scr[...] > t, 1.0, 0.0), axis=1, keepdims=True)

        hi, chi = _kth_threshold(count_gt, rowmax, small, topk)
        need = jnp.where(small, 0.0, jnp.float32(topk) - chi)
        any_tie = jnp.max(need) > 0.0

        @pl.when(jnp.logical_not(any_tie))
        def _():
            madd_scr[...] = jnp.where(sc_scr[...] > hi, 0.0, -jnp.inf)

        @pl.when(any_tie)
        def _():
            r = lax.broadcasted_iota(I32, (LANES, LANES), 0)
            c = lax.broadcasted_iota(I32, (LANES, LANES), 1)
            upper = jnp.where(r < c, 1.0, 0.0).astype(BF16)

            def tb(cix, eqb):
                c0 = pl.multiple_of(cix * LANES, LANES)
                xx = sc_scr[:, pl.ds(c0, LANES)]
                eq = xx == hi
                e = jnp.where(eq, 1.0, 0.0)
                rank = jnp.dot(e.astype(BF16), upper, preferred_element_type=F32) + eqb
                sel = jnp.logical_or(xx > hi, jnp.logical_and(eq, rank < need))
                madd_scr[:, pl.ds(c0, LANES)] = jnp.where(sel, 0.0, -jnp.inf)
                return eqb + jnp.sum(e, axis=1, keepdims=True)

            lax.fori_loop(0, lk // LANES, tb, jnp.zeros((ds, 1), F32))

        def madd_at(r0, width):
            m = madd_scr[:, pl.ds(r0, width)]
            return jnp.broadcast_to(m[None], (N_HEADS, ds, width)).reshape(rows, width)
    else:
        nbs = past // MOBA_BLOCK
        for n in range(nbs):
            mean_scr[:, n:n + 1] = jnp.mean(kbuf[slot, :, n * MOBA_BLOCK:(n + 1) * MOBA_BLOCK], axis=1, keepdims=True)
        gate = jnp.dot(qbd_ref[...], mean_scr[...].astype(BF16), preferred_element_type=F32)
        blk = lax.broadcasted_iota(I32, (rows, nbs), 1)
        ok = _top_blocks(gate, blk, nbs, n_sel)
        mv = jnp.where(ok, 0.0, -jnp.inf)
        for n in range(nbs):
            madd_scr[:, n * MOBA_BLOCK:(n + 1) * MOBA_BLOCK] = jnp.broadcast_to(mv[:, n:n + 1], (rows, MOBA_BLOCK))
        madd_scr[:, past:lk] = jnp.zeros((rows, LANES), F32)

        def madd_at(r0, width):
            return madd_scr[:, pl.ds(r0, width)]

    qbd = qbd_ref[...]
    b31 = b31_ref[...]

    def lg_body(c, m):
        r0 = pl.multiple_of(c * ck, ck)
        lg = jnp.dot(qbd, kbuf[slot, :, pl.ds(r0, ck)].astype(BF16), preferred_element_type=F32)
        lg = lg + madd_at(r0, ck) + jnp.where(c == nck - 1, btab_ref[:, 0:ck], b31)
        lg_scr[:, pl.ds(r0, ck)] = lg
        return jnp.maximum(m, jnp.max(lg, axis=1, keepdims=True))

    m = lax.fori_loop(0, nck, lg_body, jnp.full((rows, 1), NEG, F32))
    lgt = (lax.dot_general(qbd, kf_tail, nt, preferred_element_type=F32)
           + madd_at(past, LANES) + btab_ref[:, ck:ck + LANES])
    m = jnp.maximum(m, jnp.max(lgt, axis=1, keepdims=True))

    def pv_body(c, carry):
        l, acc = carry
        r0 = pl.multiple_of(c * ck, ck)
        p = jnp.exp(lg_scr[:, pl.ds(r0, ck)] - m)
        vc = vbuf[slot, :, pl.ds(r0, ck)].astype(BF16)
        return (l + jnp.sum(p, axis=1, keepdims=True),
                acc + lax.dot_general(p.astype(BF16), vc, nt, preferred_element_type=F32))

    l, acc = lax.fori_loop(0, nck, pv_body, (jnp.zeros((rows, 1), F32), jnp.zeros((rows, KVW), F32)))
    pt = jnp.exp(lgt - m)
    l = l + jnp.sum(pt, axis=1, keepdims=True)
    acc = acc + jnp.dot(pt.astype(BF16), vf_tail, preferred_element_type=F32)
    o_ref[...] = acc / l


def _sample_attn(mode, page_table, q_s, k_fresh, v_fresh, k_pool, v_pool, bias, *, topk=0, n_sel=0,
                 qi_s=None, wi_s=None, ki_fresh=None, i_pool=None):
    db, ds, _ = q_s.shape
    npages = page_table.shape[1]
    page = k_pool.shape[2]
    past = npages * page
    ck = min(1024, past)
    rows = N_HEADS * ds
    lk = past + LANES
    assert past % ck == 0 and ck >= REL_MAX_DIST + ds and ds <= LANES and past % MOBA_BLOCK == 0 and ds % 8 == 0

    qh = jnp.transpose(q_s.reshape(db, ds, N_HEADS, HEAD_DIM), (0, 2, 1, 3))
    onehot = jnp.asarray((np.arange(N_HEADS)[:, None] // GRP == np.arange(N_KV)[None, :]).astype(np.float32), BF16)
    qbd = (qh[:, :, :, None, :] * onehot[None, :, None, :, None]).reshape(db, rows, KVW)

    hq = np.arange(ds)[:, None]
    c = np.arange(ck + LANES)[None, :]
    dist = np.where(c < ck, ck + hq - c, hq - (c - ck))
    valid = np.where(c < ck, True, (c - ck) <= hq)
    btab = _bias_table(bias, dist, valid).reshape(rows, ck + LANES)
    b31 = jnp.repeat(bias[REL_BUCKETS - 1].astype(F32), ds).reshape(rows, 1)

    per = lambda shape: pl.BlockSpec((None,) + shape, lambda b, pt: (b, 0, 0))
    const = lambda a: pl.BlockSpec(a.shape, lambda b, pt: (0,) * a.ndim)
    anyspec = pl.BlockSpec(memory_space=pl.ANY)
    in_specs = [per((rows, KVW)), per((ds, KVW)), per((ds, KVW)), const(btab), const(b31)]
    args = [qbd, k_fresh, v_fresh, btab, b31]
    scratch = [pltpu.VMEM((2, KVW, past), F32), pltpu.VMEM((2, KVW, past), F32)]
    if mode == "dsa":
        qis = jnp.transpose(qi_s.reshape(db, ds, IDX_HEADS, IDX_DIM), (0, 2, 1, 3)).reshape(db, IDX_HEADS * ds, IDX_DIM)
        wic = jnp.transpose(wi_s, (0, 2, 1)).reshape(db, IDX_HEADS * ds, 1)
        in_specs += [per((IDX_HEADS * ds, IDX_DIM)), per((IDX_HEADS * ds, 1)), per((ds, IDX_DIM)), anyspec]
        args += [qis, wic, ki_fresh, i_pool]
        scratch += [pltpu.VMEM((2, IDX_DIM, past), F32), pltpu.SemaphoreType.DMA((3, 2)),
                    pltpu.VMEM((rows, lk), F32), pltpu.VMEM((ds, lk), F32), pltpu.VMEM((ds, lk), F32)]
    else:
        scratch += [pltpu.SemaphoreType.DMA((2, 2)),
                    pltpu.VMEM((rows, lk), F32), pltpu.VMEM((rows, lk), F32),
                    pltpu.VMEM((KVW, past // MOBA_BLOCK), F32)]
    in_specs += [anyspec, anyspec]
    args += [k_pool, v_pool]
    o = pl.pallas_call(
        functools.partial(_sample_kernel, mode=mode, past=past, page=page, ds=ds, ck=ck, topk=topk, n_sel=n_sel),
        grid_spec=pltpu.PrefetchScalarGridSpec(
            num_scalar_prefetch=1, grid=(db,), in_specs=in_specs,
            out_specs=pl.BlockSpec((None, rows, KVW), lambda b, pt: (b, 0, 0)),
            scratch_shapes=scratch),
        out_shape=jax.ShapeDtypeStruct((db, rows, KVW), F32),
        compiler_params=pltpu.CompilerParams(dimension_semantics=("arbitrary",), vmem_limit_bytes=VMEM_LIMIT),
        name="sample_" + mode,
    )(page_table, *args)
    o5 = o.reshape(db, N_HEADS, ds, N_KV, HEAD_DIM)
    og = o5[:, np.arange(N_HEADS), :, np.arange(N_HEADS) // GRP, :]
    return jnp.transpose(og, (1, 2, 0, 3)).reshape(db, ds, WIDTH)


def _out_kernel(x_ref, gate_ref, oa_ref, za_ref, ob_ref, zb_ref, wo_ref, fg_ref, y_ref, *, final_norm):
    za = za_ref[...]
    zb = zb_ref[...]
    ma = (oa_ref[...] * (za * jax.nn.sigmoid(za))).astype(BF16)
    mb = (ob_ref[...] * (zb * jax.nn.sigmoid(zb))).astype(BF16)
    r = (jnp.dot(ma, wo_ref[0:WIDTH, :], preferred_element_type=F32)
         + jnp.dot(mb, wo_ref[WIDTH:2 * WIDTH, :], preferred_element_type=F32))
    h = x_ref[...] + gate_ref[...] * r
    if final_norm:
        h = h * lax.rsqrt(jnp.mean(h * h, axis=-1, keepdims=True) + EPS) * fg_ref[...]
    y_ref[...] = h


def _out_call(x, gate, oa, za, ob, zb, w_o, final_g, *, per_row_mod, final_norm):
    bx, s, d = x.shape
    tm = min(512, s)
    assert s % tm == 0
    row = lambda c: pl.BlockSpec((None, tm, c), lambda b, i: (b, i, 0))
    gate_spec = row(d) if per_row_mod else pl.BlockSpec((None, 1, d), lambda b, i: (b, 0, 0))
    return pl.pallas_call(
        functools.partial(_out_kernel, final_norm=final_norm),
        grid=(bx, s // tm),
        in_specs=[row(d), gate_spec, row(WIDTH), row(WIDTH), row(WIDTH), row(WIDTH),
                  pl.BlockSpec(w_o.shape, lambda b, i: (0, 0)), pl.BlockSpec((1, d), lambda b, i: (0, 0))],
        out_specs=row(d),
        out_shape=jax.ShapeDtypeStruct((bx, s, d), F32),
        compiler_params=pltpu.CompilerParams(dimension_semantics=("arbitrary", "arbitrary"),
                                             vmem_limit_bytes=VMEM_LIMIT),
        name="gate_out_residual",
    )(x, gate, oa, za, ob, zb, w_o, final_g.reshape(1, d))


def kernel(x_prompt, x_sample, c_prompt, c_sample, cache_k_a, cache_v_a, cache_idx_k, cache_k_b, cache_v_b,
           page_table, rel_bias, norm_g, w_ada, b_ada, w_in, w_o, final_g):
    bsz, seq, d = x_prompt.shape
    dbsz, dseq, _ = x_sample.shape
    depth = w_in.shape[0]
    n_pool, page = cache_k_a.shape[1], cache_k_a.shape[2]
    past = page_table.shape[1] * page
    bias_a = rel_bias[:, :N_HEADS]
    bias_b = rel_bias[:, N_HEADS:]
    topk_p = min(TOPK_MAX, seq // 4)
    topk_s = min(TOPK_MAX, (past + dseq) // 4)
    nsel_s = min(MOBA_TOPK, past // MOBA_BLOCK)

    hp = x_prompt
    hs = x_sample.reshape(1, dbsz * dseq, d)
    outs = [[] for _ in range(10)]
    for l in range(depth):
        w_l = w_in[l]
        w_cols_p = _pick_cols(w_l, _COLS_PROMPT).astype(BF16)
        w_rows_t = jnp.transpose(_pick_cols(w_l, _ROWS_PROMPT_T)).astype(BF16)
        w_cols_s = _pick_cols(w_l, _COLS_SAMPLE).astype(BF16)
        w_o_l = w_o[l].astype(BF16)
        last = l == depth - 1

        mod = _mod_call(jnp.concatenate([c_prompt, c_sample], axis=0), w_ada[l], b_ada[l])
        shift, scale, gate = jnp.split(mod, 3, axis=-1)

        pm = lambda t: t[:bsz].reshape(bsz, 1, d)
        (qa, qi, qb, za, zb, wi, va_bf, vb_bf, kat, vat, kbt, vbt, kit, kat_bf, kbt_bf, kit_bf, means) = _proj_call(
            hp, pm(scale), pm(shift), norm_g[l], w_cols_p, w_rows_t, per_row_mod=False)
        means = jnp.transpose(means, (0, 2, 1, 3)).reshape(bsz, KVW, seq // MOBA_BLOCK)
        oa = _dsa_prompt(qi, wi, qa, kit_bf, kat_bf, va_bf, bias_a, topk_p)
        ob = _moba_prompt(qb, kbt_bf, vb_bf, means, bias_b)
        hp = _out_call(hp, pm(gate), oa, za, ob, zb, w_o_l, final_g, per_row_mod=False, final_norm=last)

        sm = lambda t: jnp.broadcast_to(t[bsz:, None, :], (dbsz, dseq, d)).reshape(1, dbsz * dseq, d)
        (qa_s, qi_s, qb_s, za_s, zb_s, wi_s, ka_s, va_s, kb_s, vb_s, ki_s) = _proj_call(
            hs, sm(scale), sm(shift), norm_g[l], w_cols_s, None, per_row_mod=True)
        sq = lambda t: t.reshape(dbsz, dseq, t.shape[-1])
        pool = lambda c: jnp.moveaxis(c[l], 1, -1).reshape(n_pool, -1, page)
        oa_s = _sample_attn("dsa", page_table, sq(qa_s), sq(ka_s), sq(va_s), pool(cache_k_a), pool(cache_v_a), bias_a,
                            topk=topk_s, qi_s=sq(qi_s), wi_s=sq(wi_s), ki_fresh=sq(ki_s), i_pool=pool(cache_idx_k))
        ob_s = _sample_attn("moba", page_table, sq(qb_s), sq(kb_s), sq(vb_s), pool(cache_k_b), pool(cache_v_b), bias_b,
                            n_sel=nsel_s)
        flat = lambda t: t.reshape(1, dbsz * dseq, WIDTH)
        hs = _out_call(hs, sm(gate), flat(oa_s), za_s, flat(ob_s), zb_s, w_o_l, final_g, per_row_mod=True, final_norm=last)

        kv_t = lambda t: jnp.transpose(t.reshape(bsz, N_KV, HEAD_DIM, seq), (0, 3, 1, 2))
        kv = lambda t: t.reshape(dbsz, dseq, N_KV, HEAD_DIM)
        for lst, val in zip(outs, (kv_t(kat), kv_t(vat), jnp.transpose(kit, (0, 2, 1)), kv_t(kbt), kv_t(vbt),
                                   kv(ka_s), kv(va_s), ki_s.reshape(dbsz, dseq, IDX_DIM), kv(kb_s), kv(vb_s))):
            lst.append(val)

    return (hp, hs.reshape(dbsz, dseq, d)) + tuple(jnp.stack(o) for o in outs)
```

```python
import functools
import math

import numpy as np
import jax
import jax.numpy as jnp
from jax import lax
from jax.experimental import pallas as pl
from jax.experimental.pallas import tpu as pltpu

F32 = jnp.float32
BF16 = jnp.bfloat16
I32 = jnp.int32

HEAD_DIM = 64
N_HEADS = 8
N_KV = 4
GRP = N_HEADS // N_KV
WIDTH = N_HEADS * HEAD_DIM
KVW = N_KV * HEAD_DIM
IDX_HEADS = 8
IDX_DIM = 64
TOPK_MAX = 256
MOBA_BLOCK = 256
MOBA_TOPK = 3
REL_BUCKETS = 32
REL_MAX_DIST = 128
EPS = 1e-6
NEG = -1e30
LANES = 128
VMEM_LIMIT = 56 * 1024 * 1024

_ORIG = (("qa", WIDTH), ("ka", KVW), ("va", KVW), ("za", WIDTH), ("qi", IDX_HEADS * IDX_DIM), ("ki", IDX_DIM),
         ("wi", IDX_HEADS), ("qb", WIDTH), ("kb", KVW), ("vb", KVW), ("zb", WIDTH))
_COLS_SAMPLE = ("qa", "ka", "va", "za", "qi", "qb", "kb", "vb", "zb", "ki", "wi")
_ROWS_PROMPT_T = ("qa", "qi", "qb", "ka", "va", "kb", "vb", "ki", "wi")
_P_ZA, _P_ZB, _P_KA, _P_KEXT, _P_KI, _P_END = 0, WIDTH, 2 * WIDTH, 2 * WIDTH + KVW, 2 * WIDTH + KVW + N_KV * LANES, \
    2 * WIDTH + KVW + N_KV * LANES + LANES


def _col_ranges():
    off, o = {}, 0
    for name, sz in _ORIG:
        off[name] = (o, o + sz)
        o += sz
    return off


def _offsets(order):
    sizes = dict(_ORIG)
    off, o = {}, 0
    for name in order:
        off[name] = (o, o + sizes[name])
        o += sizes[name]
    return off, o


def _pick_cols(w, order):
    cols = _col_ranges()
    parts = [w[:, cols[n][0]:cols[n][1]] for n in order]
    pad = (-sum(p.shape[1] for p in parts)) % LANES
    if pad:
        parts.append(jnp.zeros((w.shape[0], pad), w.dtype))
    return jnp.concatenate(parts, axis=1)


def _prompt_cols(w):
    cols = _col_ranges()
    take = lambda n: w[:, cols[n][0]:cols[n][1]]
    z = jnp.zeros((w.shape[0], HEAD_DIM), w.dtype)
    kb = take("kb")
    parts = [take("za"), take("zb"), take("ka")]
    for g in range(N_KV):
        parts += [kb[:, g * HEAD_DIM:(g + 1) * HEAD_DIM], z]
    parts += [take("ki"), z]
    return jnp.concatenate(parts, axis=1)


def _bucket_np(n):
    n = np.maximum(np.asarray(n, np.int64), 0)
    exact = REL_BUCKETS // 2

    def large(dt):
        nf = np.maximum(n, 1).astype(dt)
        return exact + (np.log(nf / dt(exact)) / dt(math.log(REL_MAX_DIST / exact)) * dt(REL_BUCKETS - exact)).astype(np.int64)

    l32, l64 = large(np.float32), large(np.float64)
    assert np.array_equal(np.minimum(l32, REL_BUCKETS - 1)[n >= exact], np.minimum(l64, REL_BUCKETS - 1)[n >= exact])
    return np.where(n < exact, n, np.minimum(l32, REL_BUCKETS - 1)).astype(np.int32)


def _bucket_index(dist, valid):
    return jnp.asarray(np.where(valid, _bucket_np(dist), -1).astype(np.int32))


def _bias_table(bias, dist, valid):
    t = jnp.transpose(bias.astype(F32)[_bucket_np(dist)], (2, 0, 1))
    return jnp.where(jnp.asarray(valid)[None], t, -jnp.inf)


def _fill_bias_tables(bias_ref, idx_ref, tab_scr, b31_scr, tq):
    lead = idx_ref.shape[:-2]
    for g in range(N_KV):
        for j in range(GRP):
            head = g * GRP + j
            for ix in np.ndindex(*lead):
                idx = idx_ref[ix]

                def pick(k, t, idx=idx, head=head):
                    return jnp.where(idx == k, bias_ref[k, head], t)

                t = lax.fori_loop(0, REL_BUCKETS, pick, jnp.full(idx.shape, -jnp.inf, F32))
                tab_scr[(g,) + ix + (slice(None), slice(j * tq, (j + 1) * tq))] = t
            b31_scr[g, :, j * tq:(j + 1) * tq] = jnp.full((1, tq), bias_ref[REL_BUCKETS - 1, head], F32)


def _mod_kernel(c_ref, w_ref, b_ref, o_ref):
    c = c_ref[...]
    s = c * jax.nn.sigmoid(c)
    o_ref[...] = jnp.dot(s, w_ref[...], preferred_element_type=F32, precision=lax.Precision.HIGHEST) + b_ref[...]


def _mod_call(c, w_ada, b_ada):
    n, d = c.shape
    d3 = w_ada.shape[1]
    tn = 512 if d3 % 512 == 0 else d3
    return pl.pallas_call(
        _mod_kernel,
        grid=(d3 // tn,),
        in_specs=[pl.BlockSpec((n, d), lambda j: (0, 0)),
                  pl.BlockSpec((d, tn), lambda j: (0, j)),
                  pl.BlockSpec((1, tn), lambda j: (0, j))],
        out_specs=pl.BlockSpec((n, tn), lambda j: (0, j)),
        out_shape=jax.ShapeDtypeStruct((n, d3), F32),
        compiler_params=pltpu.CompilerParams(dimension_semantics=("arbitrary",), vmem_limit_bytes=VMEM_LIMIT),
        name="adaln_mod",
    )(c, w_ada, b_ada.reshape(1, d3))


def _proj_kernel(*refs, prompt, tm):
    if prompt:
        (x_ref, sc_ref, sh_ref, g_ref, w_ref, wt_ref,
         za_o, zb_o, kab_o, kext_o, kib_o, mean_o,
         qat_o, qit_o, qbt_o, wit_o, kat_o, vat_o, kbt_o, vbt_o, kit_o, vatb_o, vbtb_o) = refs
    else:
        (x_ref, sc_ref, sh_ref, g_ref, w_ref,
         qa_o, qi_o, qb_o, za_o, zb_o, wi_o, ka_o, va_o, kb_o, vb_o, ki_o) = refs
    x = x_ref[...]
    y = x * lax.rsqrt(jnp.mean(x * x, axis=-1, keepdims=True) + EPS) * g_ref[...]
    h = y * (1.0 + sc_ref[...]) + sh_ref[...]
    hb = h.astype(BF16)
    qscale = HEAD_DIM ** -0.5
    iscale = IDX_DIM ** -0.5
    wscale = IDX_HEADS ** -0.5

    def cols(a, b):
        return jnp.dot(hb, w_ref[:, a:b], preferred_element_type=F32)

    if prompt:
        za_o[...] = cols(_P_ZA, _P_ZB)
        zb_o[...] = cols(_P_ZB, _P_KA)
        kab_o[...] = cols(_P_KA, _P_KEXT).astype(BF16)
        kib_o[...] = cols(_P_KI, _P_END)[:, :IDX_DIM].astype(BF16)
        pos = pl.program_id(1) * tm + lax.broadcasted_iota(I32, (tm, LANES), 0)
        lane = lax.broadcasted_iota(I32, (tm, LANES), 1)
        onehot = jnp.where(lane - HEAD_DIM == jnp.right_shift(pos, int(math.log2(MOBA_BLOCK))), 1.0, 0.0)
        for g in range(N_KV):
            kx = cols(_P_KEXT + g * LANES, _P_KEXT + (g + 1) * LANES)
            kext_o[g] = (kx + onehot).astype(BF16)
            for r in range(tm // MOBA_BLOCK):
                mean_o[r:r + 1, g * LANES:(g + 1) * LANES] = jnp.mean(
                    kx[r * MOBA_BLOCK:(r + 1) * MOBA_BLOCK], axis=0, keepdims=True)
        roff, _ = _offsets(_ROWS_PROMPT_T)
        nt = (((1,), (1,)), ((), ()))

        def rows(name):
            a, b = roff[name]
            return lax.dot_general(wt_ref[a:b, :], hb, nt, preferred_element_type=F32)

        qat_o[...] = (rows("qa") * qscale).astype(BF16)
        qit_o[...] = (rows("qi") * iscale).astype(BF16)
        qbt_o[...] = (rows("qb") * qscale).astype(BF16)
        a = roff["wi"][0]
        wit_o[...] = lax.dot_general(wt_ref[a:a + 16, :], hb, nt, preferred_element_type=F32)[:IDX_HEADS] * wscale
        kat_o[...] = rows("ka")
        kbt_o[...] = rows("kb")
        kit_o[...] = rows("ki")
        vat = rows("va")
        vbt = rows("vb")
        vat_o[...] = vat
        vbt_o[...] = vbt
        vatb_o[...] = vat.astype(BF16)
        vbtb_o[...] = vbt.astype(BF16)
    else:
        off, _ = _offsets(_COLS_SAMPLE)
        seg = lambda name: cols(*off[name])
        qa_o[...] = (seg("qa") * qscale).astype(BF16)
        qi_o[...] = (seg("qi") * iscale).astype(BF16)
        qb_o[...] = (seg("qb") * qscale).astype(BF16)
        za_o[...] = seg("za")
        zb_o[...] = seg("zb")
        ka_o[...] = seg("ka")
        va_o[...] = seg("va")
        kb_o[...] = seg("kb")
        vb_o[...] = seg("vb")
        a, _ = off["ki"]
        kiwi = cols(a, a + LANES)
        ki_o[...] = kiwi[:, :IDX_DIM]
        wi_o[...] = kiwi[:, IDX_DIM:IDX_DIM + IDX_HEADS] * wscale


def _proj_call(x, scale, shift, g, w_cols, w_rows_t, *, per_row_mod):
    bx, s, d = x.shape
    tm = min(512, s)
    assert s % tm == 0 and tm % 8 == 0
    prompt = w_rows_t is not None
    nw = w_cols.shape[1]
    row = lambda c: pl.BlockSpec((None, tm, c), lambda b, i: (b, i, 0))
    mod_spec = row(d) if per_row_mod else pl.BlockSpec((None, 1, d), lambda b, i: (b, 0, 0))
    in_specs = [row(d), mod_spec, mod_spec,
                pl.BlockSpec((1, d), lambda b, i: (0, 0)),
                pl.BlockSpec((d, nw), lambda b, i: (0, 0))]
    args = [x, scale, shift, g.reshape(1, d), w_cols]
    sds = lambda c, dt: jax.ShapeDtypeStruct((bx, s, c), dt)
    if prompt:
        assert tm % MOBA_BLOCK == 0 and MOBA_BLOCK == 256 and nw == _P_END
        in_specs.append(pl.BlockSpec(w_rows_t.shape, lambda b, i: (0, 0)))
        args.append(w_rows_t)
        col_t = lambda r: pl.BlockSpec((None, r, tm), lambda b, i: (b, 0, i))
        sds_t = lambda r, dt: jax.ShapeDtypeStruct((bx, r, s), dt)
        nblk = tm // MOBA_BLOCK
        out_shape = [sds(WIDTH, F32), sds(WIDTH, F32), sds(KVW, BF16),
                     jax.ShapeDtypeStruct((bx, N_KV, s, LANES), BF16), sds(IDX_DIM, BF16),
                     jax.ShapeDtypeStruct((bx, s // tm, nblk, N_KV * LANES), F32),
                     sds_t(WIDTH, BF16), sds_t(WIDTH, BF16), sds_t(WIDTH, BF16), sds_t(IDX_HEADS, F32),
                     sds_t(KVW, F32), sds_t(KVW, F32), sds_t(KVW, F32), sds_t(KVW, F32), sds_t(IDX_DIM, F32),
                     sds_t(KVW, BF16), sds_t(KVW, BF16)]
        out_specs = [row(WIDTH), row(WIDTH), row(KVW),
                     pl.BlockSpec((None, N_KV, tm, LANES), lambda b, i: (b, 0, i, 0)), row(IDX_DIM),
                     pl.BlockSpec((None, None, nblk, N_KV * LANES), lambda b, i: (b, i, 0, 0)),
                     col_t(WIDTH), col_t(WIDTH), col_t(WIDTH), col_t(IDX_HEADS),
                     col_t(KVW), col_t(KVW), col_t(KVW), col_t(KVW), col_t(IDX_DIM),
                     col_t(KVW), col_t(KVW)]
    else:
        out_shape = [sds(WIDTH, BF16), sds(WIDTH, BF16), sds(WIDTH, BF16), sds(WIDTH, F32), sds(WIDTH, F32),
                     sds(IDX_HEADS, F32), sds(KVW, F32), sds(KVW, F32), sds(KVW, F32), sds(KVW, F32), sds(IDX_DIM, F32)]
        out_specs = [row(WIDTH)] * 5 + [row(IDX_HEADS)] + [row(KVW)] * 4 + [row(IDX_DIM)]
    return pl.pallas_call(
        functools.partial(_proj_kernel, prompt=prompt, tm=tm),
        grid=(bx, s // tm),
        in_specs=in_specs, out_specs=out_specs, out_shape=out_shape,
        compiler_params=pltpu.CompilerParams(dimension_semantics=("arbitrary", "arbitrary"),
                                             vmem_limit_bytes=VMEM_LIMIT),
        name="norm_mod_proj",
    )(*args)


def _flash_init(m_scr, l_scr, acc_scr):
    m_scr[...] = jnp.full(m_scr.shape, NEG, F32)
    l_scr[...] = jnp.zeros(l_scr.shape, F32)
    acc_scr[...] = jnp.zeros(acc_scr.shape, F32)


def _flash_update(sts, shifts, vts, m_scr, l_scr, acc_scr):
    ps, alphas = [], []
    for g, (st, shift) in enumerate(zip(sts, shifts)):
        m_old = m_scr[g]
        mb = jnp.max(st, axis=0, keepdims=True)
        if shift is not None:
            mb = mb + shift
        m_new = jnp.maximum(m_old, mb)
        alpha = jnp.exp(m_old - m_new)
        p = jnp.exp(st - (m_new if shift is None else m_new - shift))
        l_scr[g] = alpha * l_scr[g] + jnp.sum(p, axis=0, keepdims=True)
        m_scr[g] = m_new
        ps.append(p.astype(BF16))
        alphas.append(alpha)
    for g, (p, alpha, vt) in enumerate(zip(ps, alphas, vts)):
        acc_scr[g] = alpha * acc_scr[g] + jnp.dot(vt, p, preferred_element_type=F32)


def _flash_finish(o_ref, l_scr, acc_scr, tq):
    for g in range(N_KV):
        o = jnp.transpose(acc_scr[g] / l_scr[g])
        c0 = (g % 2) * HEAD_DIM
        for j in range(GRP):
            hd = g * GRP + j
            o_ref[:, hd * HEAD_DIM:(hd + 1) * HEAD_DIM] = o[j * tq:(j + 1) * tq, c0:c0 + HEAD_DIM]


def _f2key(x):
    b = lax.bitcast_convert_type(x, I32)
    return b ^ ((b >> 31) & jnp.int32(0x7FFFFFFF))


def _key2f(k):
    return lax.bitcast_convert_type(k ^ ((k >> 31) & jnp.int32(0x7FFFFFFF)), F32)


def _kth_threshold(count_gt, vmax, small, k):
    kf = jnp.float32(k)
    lo0 = jnp.full(vmax.shape, _f2key(jnp.float32(-jnp.inf)), I32)
    hi0 = jnp.where(small, lo0, _f2key(vmax))
    st0 = (jnp.int32(0), lo0, hi0, jnp.zeros(vmax.shape, F32), small.astype(I32))

    def cond(st):
        return jnp.logical_and(st[0] < 34, jnp.min(st[4]) == 0)

    def body(st):
        it, lo, hi, chi, done = st
        mid = (lo & hi) + ((lo ^ hi) >> 1)
        conv = mid == lo
        cnt = count_gt(_key2f(mid))
        active = jnp.logical_and(done == 0, jnp.logical_not(conv))
        up = jnp.logical_and(active, cnt > kf)
        down = jnp.logical_and(active, cnt <= kf)
        lo = jnp.where(up, mid, lo)
        hi = jnp.where(down, mid, hi)
        chi = jnp.where(down, cnt, chi)
        fin = jnp.logical_or(conv, jnp.logical_and(active, cnt == kf))
        return it + 1, lo, hi, chi, jnp.where(fin, 1, done)

    _, _, hi, chi, _ = lax.while_loop(cond, body, st0)
    return _key2f(hi), chi


def _dsa_prompt_kernel(bias_ref, qit_ref, wit_ref, qat_ref, ki_ref, ka_ref, vat_ref, idx_ref, o_ref,
                       s_scr, qis_scr, qext_scr, tab_scr, b31_scr, m_scr, l_scr, acc_scr, *, tq, fw, topk):
    i = pl.program_id(1)
    q0 = i * tq
    rr = GRP * tq

    @pl.when(jnp.logical_and(pl.program_id(0) == 0, i == 0))
    def _():
        _fill_bias_tables(bias_ref, idx_ref, tab_scr, b31_scr, tq)

    qit = qit_ref[...]
    qat = qat_ref[...]
    wit = wit_ref[...]
    for h in range(IDX_HEADS):
        qis_scr[:, h * tq:(h + 1) * tq] = qit[h * IDX_DIM:(h + 1) * IDX_DIM, :]
    for g in range(N_KV):
        r0 = (g % 2) * HEAD_DIM
        qext_scr[g, HEAD_DIM - r0:2 * HEAD_DIM - r0, :] = jnp.zeros((HEAD_DIM, rr), BF16)
        for j in range(GRP):
            hd = g * GRP + j
            qext_scr[g, r0:r0 + HEAD_DIM, j * tq:(j + 1) * tq] = qat[hd * HEAD_DIM:(hd + 1) * HEAD_DIM, :]

    qpos = q0 + lax.broadcasted_iota(I32, (1, tq), 1)

    nb1 = (q0 + tq + fw - 1) // fw

    def p1(jb, vmax):
        k0 = pl.multiple_of(jb * fw, fw)
        d = jnp.dot(ki_ref[pl.ds(k0, fw), :], qis_scr[...], preferred_element_type=F32)
        acc = jnp.maximum(d[:, 0:tq], 0.0) * wit[0:1, :]
        for h in range(1, IDX_HEADS):
            acc = acc + jnp.maximum(d[:, h * tq:(h + 1) * tq], 0.0) * wit[h:h + 1, :]
        key = k0 + lax.broadcasted_iota(I32, (fw, tq), 0)
        sc = jnp.where(key <= qpos, acc, -jnp.inf)
        s_scr[pl.ds(k0, fw), :] = sc
        return jnp.maximum(vmax, jnp.max(sc, axis=0, keepdims=True))

    vmax = lax.fori_loop(0, nb1, p1, jnp.full((1, tq), -jnp.inf, F32))

    def count_gt(t):
        def cb(jb, c):
            x = s_scr[pl.ds(pl.multiple_of(jb * fw, fw), fw), :]
            one = jnp.where(x > t, 1.0, 0.0)
            return c + jnp.sum(one.reshape(fw // 64, 64, tq), axis=0)
        c = lax.fori_loop(0, nb1, cb, jnp.zeros((64, tq), F32))
        return jnp.sum(c, axis=0, keepdims=True)

    small = qpos < topk
    hi, chi = _kth_threshold(count_gt, vmax, small, topk)
    need = jnp.where(small, 0.0, jnp.float32(topk) - chi)
    any_tie = jnp.max(need) > 0.0

    _flash_init(m_scr, l_scr, acc_scr)
    nfar = jnp.maximum(q0 - LANES, 0) // fw
    near0 = nfar * (fw // LANES)

    def select(x, eqb, with_ties, width):
        gt = x > hi
        if not with_ties:
            return gt, eqb
        eq = x == hi
        e = jnp.where(eq, 1.0, 0.0)
        r = lax.broadcasted_iota(I32, (width, width), 0)
        c = lax.broadcasted_iota(I32, (width, width), 1)
        lower = jnp.where(c < r, 1.0, 0.0).astype(BF16)
        rank = jnp.dot(lower, e.astype(BF16), preferred_element_type=F32) + eqb
        sel = jnp.logical_or(gt, jnp.logical_and(eq, rank < need))
        return sel, eqb + jnp.sum(e, axis=0, keepdims=True)

    def attend(with_ties):
        def block(k0, width, eqb, t):
            sel, eqb = select(s_scr[pl.ds(k0, width), :], eqb, with_ties, width)
            madd = jnp.where(sel, 0.0, -jnp.inf)
            madd = jnp.concatenate([madd] * GRP, axis=1)
            sts, shifts, vts = [], [], []
            for g in range(N_KV):
                p0 = (g // 2) * LANES
                st = jnp.dot(ka_ref[pl.ds(k0, width), p0:p0 + LANES], qext_scr[g], preferred_element_type=F32) + madd
                sts.append(st if t is None else st + tab_scr[g, t])
                shifts.append(b31_scr[g] if t is None else None)
                vts.append(vat_ref[p0:p0 + LANES, pl.ds(k0, width)])
            _flash_update(sts, shifts, vts, m_scr, l_scr, acc_scr)
            return eqb

        def far(jb, eqb):
            return block(pl.multiple_of(jb * fw, fw), fw, eqb, None)

        def near(jj, eqb):
            return block(pl.multiple_of(jj * LANES, LANES), LANES, eqb, jnp.clip(jj - i + 2, 0, 2))

        eqb = lax.fori_loop(0, nfar, far, jnp.zeros((1, tq), F32))
        lax.fori_loop(near0, i + 1, near, eqb)

    @pl.when(any_tie)
    def _():
        attend(True)

    @pl.when(jnp.logical_not(any_tie))
    def _():
        attend(False)

    _flash_finish(o_ref, l_scr, acc_scr, tq)


def _dsa_prompt(qit, wit, qat, ki_bf, ka_bf, vat_bf, bias_a, topk):
    bx, _, s = qat.shape
    tq = LANES
    fw = min(512, s)
    rr = GRP * tq
    assert s % fw == 0 and s % tq == 0 and REL_MAX_DIST <= LANES
    c = np.arange(LANES)[:, None]
    r = np.arange(tq)[None, :]
    ones = np.ones((LANES, tq), bool)
    idx = jnp.stack([_bucket_index(np.full((LANES, tq), REL_MAX_DIST), ones),
                     _bucket_index(r - c + LANES, ones),
                     _bucket_index(r - c, (r - c) >= 0)])
    res = lambda shape: pl.BlockSpec((None,) + shape, lambda b, i: (b, 0, 0), pipeline_mode=pl.Buffered(1))
    blk_t = lambda rows: pl.BlockSpec((None, rows, tq), lambda b, i: (b, 0, i))
    return pl.pallas_call(
        functools.partial(_dsa_prompt_kernel, tq=tq, fw=fw, topk=topk),
        grid=(bx, s // tq),
        in_specs=[pl.BlockSpec(memory_space=pltpu.SMEM),
                  blk_t(IDX_HEADS * IDX_DIM), blk_t(IDX_HEADS), blk_t(WIDTH),
                  res((s, IDX_DIM)), res((s, KVW)), res((KVW, s)),
                  pl.BlockSpec(idx.shape, lambda b, i: (0, 0, 0), pipeline_mode=pl.Buffered(1))],
        out_specs=pl.BlockSpec((None, tq, WIDTH), lambda b, i: (b, i, 0)),
        out_shape=jax.ShapeDtypeStruct((bx, s, WIDTH), F32),
        scratch_shapes=[pltpu.VMEM((s, tq), F32),
                        pltpu.VMEM((IDX_DIM, IDX_HEADS * tq), BF16),
                        pltpu.VMEM((N_KV, LANES, rr), BF16),
                        pltpu.VMEM((N_KV, 3, LANES, rr), F32),
                        pltpu.VMEM((N_KV, 1, rr), F32),
                        pltpu.VMEM((N_KV, 1, rr), F32),
                        pltpu.VMEM((N_KV, 1, rr), F32),
                        pltpu.VMEM((N_KV, LANES, rr), F32)],
        compiler_params=pltpu.CompilerParams(dimension_semantics=("arbitrary", "arbitrary"),
                                             vmem_limit_bytes=VMEM_LIMIT),
        name="dsa_prompt",
    )(bias_a.astype(F32), qit, wit, qat, ki_bf, ka_bf, vat_bf, idx)


def _top_blocks(gate, blk, n_valid, n_sel, axis):
    nb = gate.shape[axis]
    gate = jnp.where(blk < n_valid, gate, -jnp.inf)
    sel = jnp.zeros(gate.shape, jnp.bool_)
    for _ in range(n_sel):
        mx = jnp.max(gate, axis=axis, keepdims=True)
        first = jnp.min(jnp.where(gate == mx, blk, nb), axis=axis, keepdims=True)
        pick = blk == first
        sel = jnp.logical_or(sel, pick)
        gate = jnp.where(pick, -jnp.inf, gate)
    return jnp.logical_and(sel, blk < n_valid)


def _moba_prompt_kernel(bias_ref, qbt_ref, kext_ref, vbt_ref, mean_ref, idx_ref, o_ref,
                        qext_scr, tab_scr, b31_scr, m_scr, l_scr, acc_scr, *, tq, nbp, n_sel):
    i = pl.program_id(1)
    q0 = i * tq
    own = q0 // MOBA_BLOCK
    par = (q0 % MOBA_BLOCK) // tq
    rr = GRP * tq

    @pl.when(jnp.logical_and(pl.program_id(0) == 0, i == 0))
    def _():
        _fill_bias_tables(bias_ref, idx_ref, tab_scr, b31_scr, tq)

    qbt = qbt_ref[...]
    blk = lax.broadcasted_iota(I32, (nbp, rr), 0)
    zeros = jnp.zeros((HEAD_DIM, rr), BF16)
    for g in range(N_KV):
        qg = jnp.concatenate([qbt[(g * GRP + j) * HEAD_DIM:(g * GRP + j + 1) * HEAD_DIM, :] for j in range(GRP)], axis=1)
        mg = mean_ref[:, g * LANES:(g + 1) * LANES].astype(BF16)
        gate = jnp.dot(mg, jnp.concatenate([qg, zeros], axis=0), preferred_element_type=F32)
        ok = jnp.logical_or(_top_blocks(gate, blk, own, n_sel, 0), blk == own)
        qext_scr[g, 0:HEAD_DIM, :] = qg
        qext_scr[g, HEAD_DIM:HEAD_DIM + nbp, :] = jnp.where(ok, 0.0, NEG).astype(BF16)

    _flash_init(m_scr, l_scr, acc_scr)

    def block(n, kind):
        c0 = pl.multiple_of(n * MOBA_BLOCK, MOBA_BLOCK)
        sts, shifts, vts = [], [], []
        for g in range(N_KV):
            st = jnp.dot(kext_ref[g, pl.ds(c0, MOBA_BLOCK), :], qext_scr[g], preferred_element_type=F32)
            p0 = (g // 2) * LANES
            sts.append(st if kind == 0 else st + tab_scr[g, par, kind - 1])
            shifts.append(b31_scr[g] if kind == 0 else None)
            vts.append(vbt_ref[p0:p0 + LANES, pl.ds(c0, MOBA_BLOCK)])
        _flash_update(sts, shifts, vts, m_scr, l_scr, acc_scr)

    def far(n, carry):
        block(n, 0)
        return carry

    lax.fori_loop(0, jnp.maximum(own - 1, 0), far, 0)

    @pl.when(own >= 1)
    def _():
        block(own - 1, 1)

    block(own, 2)
    _flash_finish(o_ref, l_scr, acc_scr, tq)


def _moba_prompt(qbt, kext, vbt_bf, means, bias_b):
    bx, _, s = qbt.shape
    tq = LANES
    nb = s // MOBA_BLOCK
    nbp = HEAD_DIM
    rr = GRP * tq
    npar = MOBA_BLOCK // tq
    assert s % MOBA_BLOCK == 0 and nb <= nbp and MOBA_BLOCK % tq == 0 and REL_MAX_DIST <= MOBA_BLOCK
    n_sel = min(MOBA_TOPK, nb - 1)
    means = jnp.pad(means, ((0, 0), (0, nbp - nb), (0, 0)))
    c = np.arange(MOBA_BLOCK)[:, None]
    r = np.arange(tq)[None, :]
    slabs = []
    for par in range(npar):
        d_own = par * tq + r - c
        slabs.append(jnp.stack([_bucket_index(d_own + MOBA_BLOCK, np.ones_like(d_own, bool)),
                                _bucket_index(d_own, d_own >= 0)]))
    idx = jnp.stack(slabs)
    res = lambda shape: pl.BlockSpec((None,) + shape, lambda b, i: (b,) + (0,) * len(shape),
                                     pipeline_mode=pl.Buffered(1))
    return pl.pallas_call(
        functools.partial(_moba_prompt_kernel, tq=tq, nbp=nbp, n_sel=n_sel),
        grid=(bx, s // tq),
        in_specs=[pl.BlockSpec(memory_space=pltpu.SMEM),
                  pl.BlockSpec((None, WIDTH, tq), lambda b, i: (b, 0, i)),
                  res((N_KV, s, LANES)), res((KVW, s)), res((nbp, N_KV * LANES)),
                  pl.BlockSpec(idx.shape, lambda b, i: (0, 0, 0, 0), pipeline_mode=pl.Buffered(1))],
        out_specs=pl.BlockSpec((None, tq, WIDTH), lambda b, i: (b, i, 0)),
        out_shape=jax.ShapeDtypeStruct((bx, s, WIDTH), F32),
        scratch_shapes=[pltpu.VMEM((N_KV, HEAD_DIM + nbp, rr), BF16),
                        pltpu.VMEM((N_KV, npar, 2, MOBA_BLOCK, rr), F32),
                        pltpu.VMEM((N_KV, 1, rr), F32),
                        pltpu.VMEM((N_KV, 1, rr), F32),
                        pltpu.VMEM((N_KV, 1, rr), F32),
                        pltpu.VMEM((N_KV, LANES, rr), F32)],
        compiler_params=pltpu.CompilerParams(dimension_semantics=("arbitrary", "arbitrary"),
                                             vmem_limit_bytes=VMEM_LIMIT),
        name="moba_prompt",
    )(bias_b.astype(F32), qbt, kext, vbt_bf, means, idx)


def _sample_kernel(*refs, mode, past, page, ds, ck, topk, n_sel):
    if mode == "dsa":
        (pt_ref, qbd_ref, kf_ref, vf_ref, btab_ref, b31_ref, qis_ref, wic_ref, kif_ref,
         ipool, kpool, vpool, o_ref, kbuf, vbuf, ibuf, sem, lg_scr, madd_scr, sc_scr) = refs
        pools, bufs = (kpool, vpool, ipool), (kbuf, vbuf, ibuf)
    else:
        (pt_ref, qbd_ref, kf_ref, vf_ref, btab_ref, b31_ref,
         kpool, vpool, o_ref, kbuf, vbuf, sem, lg_scr, madd_scr, mean_scr) = refs
        pools, bufs = (kpool, vpool), (kbuf, vbuf)
    b = pl.program_id(0)
    nseq = pl.num_programs(0)
    slot = b % 2
    npages = past // page
    nck = past // ck
    lk = past + LANES
    rows = N_HEADS * ds
    nt = (((1,), (1,)), ((), ()))

    def copies(seq, sl):
        out = []
        for a, (pool, buf) in enumerate(zip(pools, bufs)):
            def mk(pg, pool=pool, buf=buf, a=a):
                return pltpu.make_async_copy(pool.at[pt_ref[seq, pg]],
                                             buf.at[sl, :, pl.ds(pl.multiple_of(pg * page, page), page)],
                                             sem.at[a, sl])
            out.append(mk)
        return out

    def start_all(seq, sl):
        def body(pg, c):
            for mk in copies(seq, sl):
                mk(pg).start()
            return c
        lax.fori_loop(0, npages, body, 0)

    def wait_all(seq, sl):
        def body(pg, c):
            for mk in copies(seq, sl):
                mk(pg).wait()
            return c
        lax.fori_loop(0, npages, body, 0)

    @pl.when(b == 0)
    def _():
        start_all(b, slot)

    @pl.when(b + 1 < nseq)
    def _():
        start_all(b + 1, 1 - slot)

    wait_all(b, slot)

    def tail(ref):
        return jnp.concatenate([ref[...], jnp.zeros((LANES - ds, ref.shape[1]), F32)], axis=0).astype(BF16)

    kf_tail = tail(kf_ref)
    vf_tail = tail(vf_ref)
    qrow = lax.broadcasted_iota(I32, (ds, 1), 0)

    if mode == "dsa":
        qis = qis_ref[...]
        wic = wic_ref[...]

        def head_sum(d):
            t = jnp.maximum(d, 0.0) * wic
            return jnp.sum(t.reshape(IDX_HEADS, ds, d.shape[1]), axis=0)

        def sc_body(c, carry):
            r0 = pl.multiple_of(c * ck, ck)
            d = jnp.dot(qis, ibuf[slot, :, pl.ds(r0, ck)].astype(BF16), preferred_element_type=F32)
            sc_scr[:, pl.ds(r0, ck)] = head_sum(d)
            return carry

        lax.fori_loop(0, nck, sc_body, 0)
        lane = lax.broadcasted_iota(I32, (ds, LANES), 1)
        d_tail = lax.dot_general(qis, tail(kif_ref), nt, preferred_element_type=F32)
        sc_scr[:, past:lk] = jnp.where(lane <= qrow, head_sum(d_tail), -jnp.inf)

        x = sc_scr[...]
        rowmax = jnp.max(x, axis=1, keepdims=True)
        small = (past + qrow) < topk

        def count_gt(t):
            return jnp.sum(jnp.where(sc_scr[...] > t, 1.0, 0.0), axis=1, keepdims=True)

        hi, chi = _kth_threshold(count_gt, rowmax, small, topk)
        need = jnp.where(small, 0.0, jnp.float32(topk) - chi)
        any_tie = jnp.max(need) > 0.0

        @pl.when(jnp.logical_not(any_tie))
        def _():
            madd_scr[...] = jnp.where(sc_scr[...] > hi, 0.0, -jnp.inf)

        @pl.when(any_tie)
        def _():
            r = lax.broadcasted_iota(I32, (LANES, LANES), 0)
            c = lax.broadcasted_iota(I32, (LANES, LANES), 1)
            upper = jnp.where(r < c, 1.0, 0.0).astype(BF16)

            def tb(cix, eqb):
                c0 = pl.multiple_of(cix * LANES, LANES)
                xx = sc_scr[:, pl.ds(c0, LANES)]
                eq = xx == hi
                e = jnp.where(eq, 1.0, 0.0)
                rank = jnp.dot(e.astype(BF16), upper, preferred_element_type=F32) + eqb
                sel = jnp.logical_or(xx > hi, jnp.logical_and(eq, rank < need))
                madd_scr[:, pl.ds(c0, LANES)] = jnp.where(sel, 0.0, -jnp.inf)
                return eqb + jnp.sum(e, axis=1, keepdims=True)

            lax.fori_loop(0, lk // LANES, tb, jnp.zeros((ds, 1), F32))

        def madd_at(r0, width):
            m = madd_scr[:, pl.ds(r0, width)]
            return jnp.broadcast_to(m[None], (N_HEADS, ds, width)).reshape(rows, width)
    else:
        nbs = past // MOBA_BLOCK
        for n in range(nbs):
            mean_scr[:, n:n + 1] = jnp.mean(kbuf[slot, :, n * MOBA_BLOCK:(n + 1) * MOBA_BLOCK], axis=1, keepdims=True)
        gate = jnp.dot(qbd_ref[...], mean_scr[...].astype(BF16), preferred_element_type=F32)
        blk = lax.broadcasted_iota(I32, (rows, nbs), 1)
        ok = _top_blocks(gate, blk, nbs, n_sel, 1)
        mv = jnp.where(ok, 0.0, -jnp.inf)
        for n in range(nbs):
            madd_scr[:, n * MOBA_BLOCK:(n + 1) * MOBA_BLOCK] = jnp.broadcast_to(mv[:, n:n + 1], (rows, MOBA_BLOCK))
        madd_scr[:, past:lk] = jnp.zeros((rows, LANES), F32)

        def madd_at(r0, width):
            return madd_scr[:, pl.ds(r0, width)]

    qbd = qbd_ref[...]
    b31 = b31_ref[...]

    def lg_body(c, m):
        r0 = pl.multiple_of(c * ck, ck)
        lg = jnp.dot(qbd, kbuf[slot, :, pl.ds(r0, ck)].astype(BF16), preferred_element_type=F32)
        lg = lg + madd_at(r0, ck) + jnp.where(c == nck - 1, btab_ref[:, 0:ck], b31)
        lg_scr[:, pl.ds(r0, ck)] = lg
        return jnp.maximum(m, jnp.max(lg, axis=1, keepdims=True))

    m = lax.fori_loop(0, nck, lg_body, jnp.full((rows, 1), NEG, F32))
    lgt = (lax.dot_general(qbd, kf_tail, nt, preferred_element_type=F32)
           + madd_at(past, LANES) + btab_ref[:, ck:ck + LANES])
    m = jnp.maximum(m, jnp.max(lgt, axis=1, keepdims=True))

    def pv_body(c, carry):
        l, acc = carry
        r0 = pl.multiple_of(c * ck, ck)
        p = jnp.exp(lg_scr[:, pl.ds(r0, ck)] - m)
        vc = vbuf[slot, :, pl.ds(r0, ck)].astype(BF16)
        return (l + jnp.sum(p, axis=1, keepdims=True),
                acc + lax.dot_general(p.astype(BF16), vc, nt, preferred_element_type=F32))

    l, acc = lax.fori_loop(0, nck, pv_body, (jnp.zeros((rows, 1), F32), jnp.zeros((rows, KVW), F32)))
    pt = jnp.exp(lgt - m)
    l = l + jnp.sum(pt, axis=1, keepdims=True)
    acc = acc + jnp.dot(pt.astype(BF16), vf_tail, preferred_element_type=F32)
    o_ref[...] = acc / l


def _sample_attn(mode, page_table, q_s, k_fresh, v_fresh, k_pool, v_pool, bias, *, topk=0, n_sel=0,
                 qi_s=None, wi_s=None, ki_fresh=None, i_pool=None):
    db, ds, _ = q_s.shape
    npages = page_table.shape[1]
    page = k_pool.shape[2]
    past = npages * page
    ck = min(1024, past)
    rows = N_HEADS * ds
    lk = past + LANES
    assert past % ck == 0 and ck >= REL_MAX_DIST + ds and ds <= LANES and past % MOBA_BLOCK == 0 and ds % 8 == 0

    qh = jnp.transpose(q_s.reshape(db, ds, N_HEADS, HEAD_DIM), (0, 2, 1, 3))
    onehot = jnp.asarray((np.arange(N_HEADS)[:, None] // GRP == np.arange(N_KV)[None, :]).astype(np.float32), BF16)
    qbd = (qh[:, :, :, None, :] * onehot[None, :, None, :, None]).reshape(db, rows, KVW)

    hq = np.arange(ds)[:, None]
    c = np.arange(ck + LANES)[None, :]
    dist = np.where(c < ck, ck + hq - c, hq - (c - ck))
    valid = np.where(c < ck, True, (c - ck) <= hq)
    btab = _bias_table(bias, dist, valid).reshape(rows, ck + LANES)
    b31 = jnp.repeat(bias[REL_BUCKETS - 1].astype(F32), ds).reshape(rows, 1)

    per = lambda shape: pl.BlockSpec((None,) + shape, lambda b, pt: (b, 0, 0))
    const = lambda a: pl.BlockSpec(a.shape, lambda b, pt: (0,) * a.ndim)
    anyspec = pl.BlockSpec(memory_space=pl.ANY)
    in_specs = [per((rows, KVW)), per((ds, KVW)), per((ds, KVW)), const(btab), const(b31)]
    args = [qbd, k_fresh, v_fresh, btab, b31]
    scratch = [pltpu.VMEM((2, KVW, past), F32), pltpu.VMEM((2, KVW, past), F32)]
    if mode == "dsa":
        qis = jnp.transpose(qi_s.reshape(db, ds, IDX_HEADS, IDX_DIM), (0, 2, 1, 3)).reshape(db, IDX_HEADS * ds, IDX_DIM)
        wic = jnp.transpose(wi_s, (0, 2, 1)).reshape(db, IDX_HEADS * ds, 1)
        in_specs += [per((IDX_HEADS * ds, IDX_DIM)), per((IDX_HEADS * ds, 1)), per((ds, IDX_DIM)), anyspec]
        args += [qis, wic, ki_fresh, i_pool]
        scratch += [pltpu.VMEM((2, IDX_DIM, past), F32), pltpu.SemaphoreType.DMA((3, 2)),
                    pltpu.VMEM((rows, lk), F32), pltpu.VMEM((ds, lk), F32), pltpu.VMEM((ds, lk), F32)]
    else:
        scratch += [pltpu.SemaphoreType.DMA((2, 2)),
                    pltpu.VMEM((rows, lk), F32), pltpu.VMEM((rows, lk), F32),
                    pltpu.VMEM((KVW, past // MOBA_BLOCK), F32)]
    in_specs += [anyspec, anyspec]
    args += [k_pool, v_pool]
    o = pl.pallas_call(
        functools.partial(_sample_kernel, mode=mode, past=past, page=page, ds=ds, ck=ck, topk=topk, n_sel=n_sel),
        grid_spec=pltpu.PrefetchScalarGridSpec(
            num_scalar_prefetch=1, grid=(db,), in_specs=in_specs,
            out_specs=pl.BlockSpec((None, rows, KVW), lambda b, pt: (b, 0, 0)),
            scratch_shapes=scratch),
        out_shape=jax.ShapeDtypeStruct((db, rows, KVW), F32),
        compiler_params=pltpu.CompilerParams(dimension_semantics=("arbitrary",), vmem_limit_bytes=VMEM_LIMIT),
        name="sample_" + mode,
    )(page_table, *args)
    o5 = o.reshape(db, N_HEADS, ds, N_KV, HEAD_DIM)
    og = o5[:, np.arange(N_HEADS), :, np.arange(N_HEADS) // GRP, :]
    return jnp.transpose(og, (1, 2, 0, 3)).reshape(db, ds, WIDTH)


def _out_kernel(x_ref, gate_ref, oa_ref, za_ref, ob_ref, zb_ref, wo_ref, fg_ref, y_ref, *, final_norm):
    za = za_ref[...]
    zb = zb_ref[...]
    ma = (oa_ref[...] * (za * jax.nn.sigmoid(za))).astype(BF16)
    mb = (ob_ref[...] * (zb * jax.nn.sigmoid(zb))).astype(BF16)
    r = (jnp.dot(ma, wo_ref[0:WIDTH, :], preferred_element_type=F32)
         + jnp.dot(mb, wo_ref[WIDTH:2 * WIDTH, :], preferred_element_type=F32))
    h = x_ref[...] + gate_ref[...] * r
    if final_norm:
        h = h * lax.rsqrt(jnp.mean(h * h, axis=-1, keepdims=True) + EPS) * fg_ref[...]
    y_ref[...] = h


def _out_call(x, gate, oa, za, ob, zb, w_o, final_g, *, per_row_mod, final_norm):
    bx, s, d = x.shape
    tm = min(512, s)
    assert s % tm == 0
    row = lambda c: pl.BlockSpec((None, tm, c), lambda b, i: (b, i, 0))
    gate_spec = row(d) if per_row_mod else pl.BlockSpec((None, 1, d), lambda b, i: (b, 0, 0))
    return pl.pallas_call(
        functools.partial(_out_kernel, final_norm=final_norm),
        grid=(bx, s // tm),
        in_specs=[row(d), gate_spec, row(WIDTH), row(WIDTH), row(WIDTH), row(WIDTH),
                  pl.BlockSpec(w_o.shape, lambda b, i: (0, 0)), pl.BlockSpec((1, d), lambda b, i: (0, 0))],
        out_specs=row(d),
        out_shape=jax.ShapeDtypeStruct((bx, s, d), F32),
        compiler_params=pltpu.CompilerParams(dimension_semantics=("arbitrary", "arbitrary"),
                                             vmem_limit_bytes=VMEM_LIMIT),
        name="gate_out_residual",
    )(x, gate, oa, za, ob, zb, w_o, final_g.reshape(1, d))


def kernel(x_prompt, x_sample, c_prompt, c_sample, cache_k_a, cache_v_a, cache_idx_k, cache_k_b, cache_v_b,
           page_table, rel_bias, norm_g, w_ada, b_ada, w_in, w_o, final_g):
    bsz, seq, d = x_prompt.shape
    dbsz, dseq, _ = x_sample.shape
    depth = w_in.shape[0]
    n_pool, page = cache_k_a.shape[1], cache_k_a.shape[2]
    past = page_table.shape[1] * page
    bias_a = rel_bias[:, :N_HEADS]
    bias_b = rel_bias[:, N_HEADS:]
    topk_p = min(TOPK_MAX, seq // 4)
    topk_s = min(TOPK_MAX, (past + dseq) // 4)
    nsel_s = min(MOBA_TOPK, past // MOBA_BLOCK)

    hp = x_prompt
    hs = x_sample.reshape(1, dbsz * dseq, d)
    outs = [[] for _ in range(10)]
    for l in range(depth):
        w_l = w_in[l]
        w_cols_p = _prompt_cols(w_l).astype(BF16)
        w_rows_t = jnp.transpose(_pick_cols(w_l, _ROWS_PROMPT_T)).astype(BF16)
        w_cols_s = _pick_cols(w_l, _COLS_SAMPLE).astype(BF16)
        w_o_l = w_o[l].astype(BF16)
        last = l == depth - 1

        mod = _mod_call(jnp.concatenate([c_prompt, c_sample], axis=0), w_ada[l], b_ada[l])
        shift, scale, gate = jnp.split(mod, 3, axis=-1)

        pm = lambda t: t[:bsz].reshape(bsz, 1, d)
        (za, zb, ka_bf, kext, ki_bf, means, qat, qit, qbt, wit, kat, vat, kbt, vbt, kit, vat_bf, vbt_bf) = _proj_call(
            hp, pm(scale), pm(shift), norm_g[l], w_cols_p, w_rows_t, per_row_mod=False)
        oa = _dsa_prompt(qit, wit, qat, ki_bf, ka_bf, vat_bf, bias_a, topk_p)
        ob = _moba_prompt(qbt, kext, vbt_bf, means.reshape(bsz, seq // MOBA_BLOCK, N_KV * LANES), bias_b)
        hp = _out_call(hp, pm(gate), oa, za, ob, zb, w_o_l, final_g, per_row_mod=False, final_norm=last)

        sm = lambda t: jnp.broadcast_to(t[bsz:, None, :], (dbsz, dseq, d)).reshape(1, dbsz * dseq, d)
        (qa_s, qi_s, qb_s, za_s, zb_s, wi_s, ka_s, va_s, kb_s, vb_s, ki_s) = _proj_call(
            hs, sm(scale), sm(shift), norm_g[l], w_cols_s, None, per_row_mod=True)
        sq = lambda t: t.reshape(dbsz, dseq, t.shape[-1])
        pool = lambda c: jnp.moveaxis(c[l], 1, -1).reshape(n_pool, -1, page)
        oa_s = _sample_attn("dsa", page_table, sq(qa_s), sq(ka_s), sq(va_s), pool(cache_k_a), pool(cache_v_a), bias_a,
                            topk=topk_s, qi_s=sq(qi_s), wi_s=sq(wi_s), ki_fresh=sq(ki_s), i_pool=pool(cache_idx_k))
        ob_s = _sample_attn("moba", page_table, sq(qb_s), sq(kb_s), sq(vb_s), pool(cache_k_b), pool(cache_v_b), bias_b,
                            n_sel=nsel_s)
        flat = lambda t: t.reshape(1, dbsz * dseq, WIDTH)
        hs = _out_call(hs, sm(gate), flat(oa_s), za_s, flat(ob_s), zb_s, w_o_l, final_g, per_row_mod=True, final_norm=last)

        kv_t = lambda t: jnp.transpose(t.reshape(bsz, N_KV, HEAD_DIM, seq), (0, 3, 1, 2))
        kv = lambda t: t.reshape(dbsz, dseq, N_KV, HEAD_DIM)
        for lst, val in zip(outs, (kv_t(kat), kv_t(vat), jnp.transpose(kit, (0, 2, 1)), kv_t(kbt), kv_t(vbt),
                                   kv(ka_s), kv(va_s), ki_s.reshape(dbsz, dseq, IDX_DIM), kv(kb_s), kv(vb_s))):
            lst.append(val)

    return (hp, hs.reshape(dbsz, dseq, d)) + tuple(jnp.stack(o) for o in outs)
```

```python
import functools
import math

import numpy as np
import jax
import jax.numpy as jnp
from jax import lax
from jax.experimental import pallas as pl
from jax.experimental.pallas import tpu as pltpu

F32 = jnp.float32
BF16 = jnp.bfloat16
I32 = jnp.int32

HEAD_DIM = 64
N_HEADS = 8
N_KV = 4
GRP = N_HEADS // N_KV
WIDTH = N_HEADS * HEAD_DIM
KVW = N_KV * HEAD_DIM
IDX_HEADS = 8
IDX_DIM = 64
TOPK_MAX = 256
MOBA_BLOCK = 256
MOBA_TOPK = 3
REL_BUCKETS = 32
REL_MAX_DIST = 128
EPS = 1e-6
NEG = -1e30
LANES = 128
VMEM_LIMIT = 56 * 1024 * 1024

_ORIG = (("qa", WIDTH), ("ka", KVW), ("va", KVW), ("za", WIDTH), ("qi", IDX_HEADS * IDX_DIM), ("ki", IDX_DIM),
         ("wi", IDX_HEADS), ("qb", WIDTH), ("kb", KVW), ("vb", KVW), ("zb", WIDTH))
_COLS_SAMPLE = ("qa", "ka", "va", "za", "qi", "qb", "kb", "vb", "zb", "ki", "wi")
_ROWS_PROMPT_T = ("qa", "qi", "qb", "ka", "va", "kb", "vb", "ki", "wi")
_P_ZA, _P_ZB, _P_KA, _P_KEXT, _P_KI, _P_END = 0, WIDTH, 2 * WIDTH, 2 * WIDTH + KVW, 2 * WIDTH + KVW + N_KV * LANES, \
    2 * WIDTH + KVW + N_KV * LANES + LANES


def _col_ranges():
    off, o = {}, 0
    for name, sz in _ORIG:
        off[name] = (o, o + sz)
        o += sz
    return off


def _offsets(order):
    sizes = dict(_ORIG)
    off, o = {}, 0
    for name in order:
        off[name] = (o, o + sizes[name])
        o += sizes[name]
    return off, o


def _pick_cols(w, order):
    cols = _col_ranges()
    parts = [w[:, cols[n][0]:cols[n][1]] for n in order]
    pad = (-sum(p.shape[1] for p in parts)) % LANES
    if pad:
        parts.append(jnp.zeros((w.shape[0], pad), w.dtype))
    return jnp.concatenate(parts, axis=1)


def _prompt_cols(w):
    cols = _col_ranges()
    take = lambda n: w[:, cols[n][0]:cols[n][1]]
    z = jnp.zeros((w.shape[0], HEAD_DIM), w.dtype)
    kb = take("kb")
    parts = [take("za"), take("zb"), take("ka")]
    for g in range(N_KV):
        parts += [kb[:, g * HEAD_DIM:(g + 1) * HEAD_DIM], z]
    parts += [take("ki"), z]
    return jnp.concatenate(parts, axis=1)


def _bucket_np(n):
    n = np.maximum(np.asarray(n, np.int64), 0)
    exact = REL_BUCKETS // 2

    def large(dt):
        nf = np.maximum(n, 1).astype(dt)
        return exact + (np.log(nf / dt(exact)) / dt(math.log(REL_MAX_DIST / exact)) * dt(REL_BUCKETS - exact)).astype(np.int64)

    l32, l64 = large(np.float32), large(np.float64)
    assert np.array_equal(np.minimum(l32, REL_BUCKETS - 1)[n >= exact], np.minimum(l64, REL_BUCKETS - 1)[n >= exact])
    return np.where(n < exact, n, np.minimum(l32, REL_BUCKETS - 1)).astype(np.int32)


def _bucket_index(dist, valid):
    return jnp.asarray(np.where(valid, _bucket_np(dist), -1).astype(np.int32))


def _bias_table(bias, dist, valid):
    t = jnp.transpose(bias.astype(F32)[_bucket_np(dist)], (2, 0, 1))
    return jnp.where(jnp.asarray(valid)[None], t, -jnp.inf)


def _fill_bias_tables(bias_ref, idx_ref, tab_scr, b31_scr, tq):
    lead = idx_ref.shape[:-2]
    for g in range(N_KV):
        for j in range(GRP):
            head = g * GRP + j
            for ix in np.ndindex(*lead):
                idx = idx_ref[ix]

                def pick(k, t, idx=idx, head=head):
                    return jnp.where(idx == k, bias_ref[k, head], t)

                t = lax.fori_loop(0, REL_BUCKETS, pick, jnp.full(idx.shape, -jnp.inf, F32))
                tab_scr[(g,) + ix + (slice(None), slice(j * tq, (j + 1) * tq))] = t
            b31_scr[g, :, j * tq:(j + 1) * tq] = jnp.full((1, tq), bias_ref[REL_BUCKETS - 1, head], F32)


def _mod_kernel(c_ref, w_ref, b_ref, o_ref):
    c = c_ref[...]
    s = c * jax.nn.sigmoid(c)
    o_ref[...] = jnp.dot(s, w_ref[...], preferred_element_type=F32, precision=lax.Precision.HIGHEST) + b_ref[...]


def _mod_call(c, w_ada, b_ada):
    n, d = c.shape
    d3 = w_ada.shape[1]
    tn = 512 if d3 % 512 == 0 else d3
    return pl.pallas_call(
        _mod_kernel,
        grid=(d3 // tn,),
        in_specs=[pl.BlockSpec((n, d), lambda j: (0, 0)),
                  pl.BlockSpec((d, tn), lambda j: (0, j)),
                  pl.BlockSpec((1, tn), lambda j: (0, j))],
        out_specs=pl.BlockSpec((n, tn), lambda j: (0, j)),
        out_shape=jax.ShapeDtypeStruct((n, d3), F32),
        compiler_params=pltpu.CompilerParams(dimension_semantics=("arbitrary",), vmem_limit_bytes=VMEM_LIMIT),
        name="adaln_mod",
    )(c, w_ada, b_ada.reshape(1, d3))


def _proj_kernel(*refs, prompt, tm):
    if prompt:
        (x_ref, sc_ref, sh_ref, g_ref, w_ref, wt_ref,
         za_o, zb_o, kab_o, kext_o, kib_o, mean_o,
         qat_o, qit_o, qbt_o, wit_o, kat_o, vat_o, kbt_o, vbt_o, kit_o, vatb_o, vbtb_o) = refs
    else:
        (x_ref, sc_ref, sh_ref, g_ref, w_ref,
         qa_o, qi_o, qb_o, za_o, zb_o, wi_o, ka_o, va_o, kb_o, vb_o, ki_o) = refs
    x = x_ref[...]
    y = x * lax.rsqrt(jnp.mean(x * x, axis=-1, keepdims=True) + EPS) * g_ref[...]
    h = y * (1.0 + sc_ref[...]) + sh_ref[...]
    hb = h.astype(BF16)
    qscale = HEAD_DIM ** -0.5
    iscale = IDX_DIM ** -0.5
    wscale = IDX_HEADS ** -0.5

    def cols(a, b):
        return jnp.dot(hb, w_ref[:, a:b], preferred_element_type=F32)

    if prompt:
        za_o[...] = cols(_P_ZA, _P_ZB)
        zb_o[...] = cols(_P_ZB, _P_KA)
        kab_o[...] = cols(_P_KA, _P_KEXT).astype(BF16)
        kib_o[...] = cols(_P_KI, _P_END)[:, :IDX_DIM].astype(BF16)
        pos = pl.program_id(1) * tm + lax.broadcasted_iota(I32, (tm, LANES), 0)
        lane = lax.broadcasted_iota(I32, (tm, LANES), 1)
        onehot = jnp.where(lane - HEAD_DIM == jnp.right_shift(pos, int(math.log2(MOBA_BLOCK))), 1.0, 0.0)
        for g in range(N_KV):
            kx = cols(_P_KEXT + g * LANES, _P_KEXT + (g + 1) * LANES)
            kext_o[g] = (kx + onehot).astype(BF16)
            for r in range(tm // MOBA_BLOCK):
                mean_o[r:r + 1, g * LANES:(g + 1) * LANES] = jnp.mean(
                    kx[r * MOBA_BLOCK:(r + 1) * MOBA_BLOCK], axis=0, keepdims=True)
        roff, _ = _offsets(_ROWS_PROMPT_T)
        nt = (((1,), (1,)), ((), ()))

        def rows(name):
            a, b = roff[name]
            return lax.dot_general(wt_ref[a:b, :], hb, nt, preferred_element_type=F32)

        qat_o[...] = (rows("qa") * qscale).astype(BF16)
        qit_o[...] = (rows("qi") * iscale).astype(BF16)
        qbt_o[...] = (rows("qb") * qscale).astype(BF16)
        a = roff["wi"][0]
        wit_o[...] = lax.dot_general(wt_ref[a:a + 16, :], hb, nt, preferred_element_type=F32)[:IDX_HEADS] * wscale
        kat_o[...] = rows("ka")
        kbt_o[...] = rows("kb")
        kit_o[...] = rows("ki")
        vat = rows("va")
        vbt = rows("vb")
        vat_o[...] = vat
        vbt_o[...] = vbt
        vatb_o[...] = vat.astype(BF16)
        vbtb_o[...] = vbt.astype(BF16)
    else:
        off, _ = _offsets(_COLS_SAMPLE)
        seg = lambda name: cols(*off[name])
        qa_o[...] = (seg("qa") * qscale).astype(BF16)
        qi_o[...] = (seg("qi") * iscale).astype(BF16)
        qb_o[...] = (seg("qb") * qscale).astype(BF16)
        za_o[...] = seg("za")
        zb_o[...] = seg("zb")
        ka_o[...] = seg("ka")
        va_o[...] = seg("va")
        kb_o[...] = seg("kb")
        vb_o[...] = seg("vb")
        a, _ = off["ki"]
        kiwi = cols(a, a + LANES)
        ki_o[...] = kiwi[:, :IDX_DIM]
        wi_o[...] = kiwi[:, IDX_DIM:IDX_DIM + IDX_HEADS] * wscale


def _proj_call(x, scale, shift, g, w_cols, w_rows_t, *, per_row_mod):
    bx, s, d = x.shape
    tm = min(512, s)
    assert s % tm == 0 and tm % 8 == 0
    prompt = w_rows_t is not None
    nw = w_cols.shape[1]
    row = lambda c: pl.BlockSpec((None, tm, c), lambda b, i: (b, i, 0))
    mod_spec = row(d) if per_row_mod else pl.BlockSpec((None, 1, d), lambda b, i: (b, 0, 0))
    in_specs = [row(d), mod_spec, mod_spec,
                pl.BlockSpec((1, d), lambda b, i: (0, 0)),
                pl.BlockSpec((d, nw), lambda b, i: (0, 0))]
    args = [x, scale, shift, g.reshape(1, d), w_cols]
    sds = lambda c, dt: jax.ShapeDtypeStruct((bx, s, c), dt)
    if prompt:
        assert tm % MOBA_BLOCK == 0 and MOBA_BLOCK == 256 and nw == _P_END
        in_specs.append(pl.BlockSpec(w_rows_t.shape, lambda b, i: (0, 0)))
        args.append(w_rows_t)
        col_t = lambda r: pl.BlockSpec((None, r, tm), lambda b, i: (b, 0, i))
        sds_t = lambda r, dt: jax.ShapeDtypeStruct((bx, r, s), dt)
        nblk = tm // MOBA_BLOCK
        out_shape = [sds(WIDTH, F32), sds(WIDTH, F32), sds(KVW, BF16),
                     jax.ShapeDtypeStruct((bx, N_KV, s, LANES), BF16), sds(IDX_DIM, BF16),
                     jax.ShapeDtypeStruct((bx, s // tm, nblk, N_KV * LANES), F32),
                     sds_t(WIDTH, BF16), sds_t(WIDTH, BF16), sds_t(WIDTH, BF16), sds_t(IDX_HEADS, F32),
                     sds_t(KVW, F32), sds_t(KVW, F32), sds_t(KVW, F32), sds_t(KVW, F32), sds_t(IDX_DIM, F32),
                     sds_t(KVW, BF16), sds_t(KVW, BF16)]
        out_specs = [row(WIDTH), row(WIDTH), row(KVW),
                     pl.BlockSpec((None, N_KV, tm, LANES), lambda b, i: (b, 0, i, 0)), row(IDX_DIM),
                     pl.BlockSpec((None, None, nblk, N_KV * LANES), lambda b, i: (b, i, 0, 0)),
                     col_t(WIDTH), col_t(WIDTH), col_t(WIDTH), col_t(IDX_HEADS),
                     col_t(KVW), col_t(KVW), col_t(KVW), col_t(KVW), col_t(IDX_DIM),
                     col_t(KVW), col_t(KVW)]
    else:
        out_shape = [sds(WIDTH, BF16), sds(WIDTH, BF16), sds(WIDTH, BF16), sds(WIDTH, F32), sds(WIDTH, F32),
                     sds(IDX_HEADS, F32), sds(KVW, F32), sds(KVW, F32), sds(KVW, F32), sds(KVW, F32), sds(IDX_DIM, F32)]
        out_specs = [row(WIDTH)] * 5 + [row(IDX_HEADS)] + [row(KVW)] * 4 + [row(IDX_DIM)]
    return pl.pallas_call(
        functools.partial(_proj_kernel, prompt=prompt, tm=tm),
        grid=(bx, s // tm),
        in_specs=in_specs, out_specs=out_specs, out_shape=out_shape,
        compiler_params=pltpu.CompilerParams(dimension_semantics=("arbitrary", "arbitrary"),
                                             vmem_limit_bytes=VMEM_LIMIT),
        name="norm_mod_proj",
    )(*args)


ONES_ROWS = 16


def _flash_init(m_scr, acc_scr):
    m_scr[...] = jnp.full(m_scr.shape, NEG, F32)
    acc_scr[...] = jnp.zeros(acc_scr.shape, F32)


def _flash_update(sts, shifts, vts, m_scr, acc_scr):
    ps, alphas = [], []
    for g, (st, shift) in enumerate(zip(sts, shifts)):
        m_old = m_scr[g]
        mb = jnp.max(st, axis=0, keepdims=True)
        if shift is not None:
            mb = mb + shift
        m_new = jnp.maximum(m_old, mb)
        m_scr[g] = m_new
        alphas.append(jnp.exp(m_old - m_new))
        ps.append(jnp.exp((st - (m_new if shift is None else m_new - shift)).astype(BF16)))
    for g, (p, alpha, vt) in enumerate(zip(ps, alphas, vts)):
        vt1 = jnp.concatenate([vt, jnp.ones((ONES_ROWS, vt.shape[1]), BF16)], axis=0)
        acc_scr[g] = alpha * acc_scr[g] + jnp.dot(vt1, p, preferred_element_type=F32)


def _flash_finish(o_ref, acc_scr, tq):
    for g in range(N_KV):
        acc = acc_scr[g]
        o = jnp.transpose(acc[:LANES] / acc[LANES:LANES + 1])
        c0 = (g % 2) * HEAD_DIM
        for j in range(GRP):
            hd = g * GRP + j
            o_ref[:, hd * HEAD_DIM:(hd + 1) * HEAD_DIM] = o[j * tq:(j + 1) * tq, c0:c0 + HEAD_DIM]


def _f2key(x):
    b = lax.bitcast_convert_type(x, I32)
    return b ^ ((b >> 31) & jnp.int32(0x7FFFFFFF))


def _key2f(k):
    return lax.bitcast_convert_type(k ^ ((k >> 31) & jnp.int32(0x7FFFFFFF)), F32)


def _kth_threshold(count_gt, vmax, small, k):
    kf = jnp.float32(k)
    lo0 = jnp.full(vmax.shape, _f2key(jnp.float32(-jnp.inf)), I32)
    hi0 = jnp.where(small, lo0, _f2key(vmax))
    st0 = (jnp.int32(0), lo0, hi0, jnp.zeros(vmax.shape, F32), small.astype(I32))

    def cond(st):
        return jnp.logical_and(st[0] < 34, jnp.min(st[4]) == 0)

    def body(st):
        it, lo, hi, chi, done = st
        mid = (lo & hi) + ((lo ^ hi) >> 1)
        conv = mid == lo
        cnt = count_gt(_key2f(mid))
        active = jnp.logical_and(done == 0, jnp.logical_not(conv))
        up = jnp.logical_and(active, cnt > kf)
        down = jnp.logical_and(active, cnt <= kf)
        lo = jnp.where(up, mid, lo)
        hi = jnp.where(down, mid, hi)
        chi = jnp.where(down, cnt, chi)
        fin = jnp.logical_or(conv, jnp.logical_and(active, cnt == kf))
        return it + 1, lo, hi, chi, jnp.where(fin, 1, done)

    _, _, hi, chi, _ = lax.while_loop(cond, body, st0)
    return _key2f(hi), chi


def _dsa_prompt_kernel(bias_ref, qit_ref, wit_ref, qat_ref, ki_ref, ka_ref, vat_ref, idx_ref, o_ref,
                       s_scr, qis_scr, qext_scr, tab_scr, b31_scr, m_scr, acc_scr, *, tq, fw, topk):
    i = pl.program_id(1)
    q0 = i * tq
    rr = GRP * tq

    @pl.when(jnp.logical_and(pl.program_id(0) == 0, i == 0))
    def _():
        _fill_bias_tables(bias_ref, idx_ref, tab_scr, b31_scr, tq)

    qit = qit_ref[...]
    qat = qat_ref[...]
    wit = wit_ref[...]
    for h in range(IDX_HEADS):
        qis_scr[:, h * tq:(h + 1) * tq] = qit[h * IDX_DIM:(h + 1) * IDX_DIM, :]
    for g in range(N_KV):
        r0 = (g % 2) * HEAD_DIM
        qext_scr[g, HEAD_DIM - r0:2 * HEAD_DIM - r0, :] = jnp.zeros((HEAD_DIM, rr), BF16)
        for j in range(GRP):
            hd = g * GRP + j
            qext_scr[g, r0:r0 + HEAD_DIM, j * tq:(j + 1) * tq] = qat[hd * HEAD_DIM:(hd + 1) * HEAD_DIM, :]

    qpos = q0 + lax.broadcasted_iota(I32, (1, tq), 1)

    nb1 = (q0 + tq + fw - 1) // fw

    def p1(jb, vmax):
        k0 = pl.multiple_of(jb * fw, fw)
        d = jnp.dot(ki_ref[pl.ds(k0, fw), :], qis_scr[...], preferred_element_type=F32)
        acc = jnp.maximum(d[:, 0:tq], 0.0) * wit[0:1, :]
        for h in range(1, IDX_HEADS):
            acc = acc + jnp.maximum(d[:, h * tq:(h + 1) * tq], 0.0) * wit[h:h + 1, :]
        key = k0 + lax.broadcasted_iota(I32, (fw, tq), 0)
        sc = jnp.where(key <= qpos, acc, -jnp.inf)
        s_scr[pl.ds(k0, fw), :] = sc
        return jnp.maximum(vmax, jnp.max(sc, axis=0, keepdims=True))

    vmax = lax.fori_loop(0, nb1, p1, jnp.full((1, tq), -jnp.inf, F32))

    def count_gt(t):
        def cb(jb, c):
            x = s_scr[pl.ds(pl.multiple_of(jb * fw, fw), fw), :]
            one = jnp.where(x > t, 1.0, 0.0)
            return c + jnp.sum(one.reshape(fw // 64, 64, tq), axis=0)
        c = lax.fori_loop(0, nb1, cb, jnp.zeros((64, tq), F32))
        return jnp.sum(c, axis=0, keepdims=True)

    small = qpos < topk
    hi, chi = _kth_threshold(count_gt, vmax, small, topk)
    need = jnp.where(small, 0.0, jnp.float32(topk) - chi)
    any_tie = jnp.max(need) > 0.0

    _flash_init(m_scr, acc_scr)
    nfar = jnp.maximum(q0 - LANES, 0) // fw
    near0 = nfar * (fw // LANES)

    def select(x, eqb, with_ties, width):
        gt = x > hi
        if not with_ties:
            return gt, eqb
        eq = x == hi
        e = jnp.where(eq, 1.0, 0.0)
        r = lax.broadcasted_iota(I32, (width, width), 0)
        c = lax.broadcasted_iota(I32, (width, width), 1)
        lower = jnp.where(c < r, 1.0, 0.0).astype(BF16)
        rank = jnp.dot(lower, e.astype(BF16), preferred_element_type=F32) + eqb
        sel = jnp.logical_or(gt, jnp.logical_and(eq, rank < need))
        return sel, eqb + jnp.sum(e, axis=0, keepdims=True)

    def attend(with_ties):
        def masked_qk(k0, width, eqb):
            sel, eqb = select(s_scr[pl.ds(k0, width), :], eqb, with_ties, width)
            madd = jnp.where(sel, 0.0, -jnp.inf)
            madd = jnp.concatenate([madd] * GRP, axis=1)
            sts = tuple(jnp.dot(ka_ref[pl.ds(k0, width), (g // 2) * LANES:(g // 2 + 1) * LANES], qext_scr[g],
                                preferred_element_type=F32) + madd for g in range(N_KV))
            return sts, eqb

        def softmax_pv(k0, width, sts, t):
            vts = [vat_ref[(g // 2) * LANES:(g // 2 + 1) * LANES, pl.ds(k0, width)] for g in range(N_KV)]
            if t is None:
                _flash_update(sts, [b31_scr[g] for g in range(N_KV)], vts, m_scr, acc_scr)
            else:
                _flash_update([st + tab_scr[g, t] for g, st in enumerate(sts)], [None] * N_KV, vts,
                              m_scr, acc_scr)

        def far(jb, eqb):
            k0 = pl.multiple_of(jb * fw, fw)
            sts, eqb = masked_qk(k0, fw, eqb)
            softmax_pv(k0, fw, sts, None)
            return eqb

        def near(jj, eqb):
            k0 = pl.multiple_of(jj * LANES, LANES)
            sts, eqb = masked_qk(k0, LANES, eqb)
            softmax_pv(k0, LANES, sts, jnp.clip(jj - i + 2, 0, 2))
            return eqb

        eqb = lax.fori_loop(0, nfar, far, jnp.zeros((1, tq), F32))
        lax.fori_loop(near0, i + 1, near, eqb)

    @pl.when(any_tie)
    def _():
        attend(True)

    @pl.when(jnp.logical_not(any_tie))
    def _():
        attend(False)

    _flash_finish(o_ref, acc_scr, tq)


def _dsa_prompt(qit, wit, qat, ki_bf, ka_bf, vat_bf, bias_a, topk):
    bx, _, s = qat.shape
    tq = LANES
    fw = min(512, s)
    rr = GRP * tq
    assert s % fw == 0 and s % tq == 0 and REL_MAX_DIST <= LANES
    c = np.arange(LANES)[:, None]
    r = np.arange(tq)[None, :]
    ones = np.ones((LANES, tq), bool)
    idx = jnp.stack([_bucket_index(np.full((LANES, tq), REL_MAX_DIST), ones),
                     _bucket_index(r - c + LANES, ones),
                     _bucket_index(r - c, (r - c) >= 0)])
    res = lambda shape: pl.BlockSpec((None,) + shape, lambda b, i: (b, 0, 0), pipeline_mode=pl.Buffered(1))
    blk_t = lambda rows: pl.BlockSpec((None, rows, tq), lambda b, i: (b, 0, i))
    return pl.pallas_call(
        functools.partial(_dsa_prompt_kernel, tq=tq, fw=fw, topk=topk),
        grid=(bx, s // tq),
        in_specs=[pl.BlockSpec(memory_space=pltpu.SMEM),
                  blk_t(IDX_HEADS * IDX_DIM), blk_t(IDX_HEADS), blk_t(WIDTH),
                  res((s, IDX_DIM)), res((s, KVW)), res((KVW, s)),
                  pl.BlockSpec(idx.shape, lambda b, i: (0, 0, 0), pipeline_mode=pl.Buffered(1))],
        out_specs=pl.BlockSpec((None, tq, WIDTH), lambda b, i: (b, i, 0)),
        out_shape=jax.ShapeDtypeStruct((bx, s, WIDTH), F32),
        scratch_shapes=[pltpu.VMEM((s, tq), F32),
                        pltpu.VMEM((IDX_DIM, IDX_HEADS * tq), BF16),
                        pltpu.VMEM((N_KV, LANES, rr), BF16),
                        pltpu.VMEM((N_KV, 3, LANES, rr), F32),
                        pltpu.VMEM((N_KV, 1, rr), F32),
                        pltpu.VMEM((N_KV, 1, rr), F32),
                        pltpu.VMEM((N_KV, LANES + ONES_ROWS, rr), F32)],
        compiler_params=pltpu.CompilerParams(dimension_semantics=("arbitrary", "arbitrary"),
                                             vmem_limit_bytes=VMEM_LIMIT),
        name="dsa_prompt",
    )(bias_a.astype(F32), qit, wit, qat, ki_bf, ka_bf, vat_bf, idx)


def _top_blocks(gate, blk, n_valid, n_sel, axis):
    nb = gate.shape[axis]
    gate = jnp.where(blk < n_valid, gate, -jnp.inf)
    sel = jnp.zeros(gate.shape, jnp.bool_)
    for _ in range(n_sel):
        mx = jnp.max(gate, axis=axis, keepdims=True)
        first = jnp.min(jnp.where(gate == mx, blk, nb), axis=axis, keepdims=True)
        pick = blk == first
        sel = jnp.logical_or(sel, pick)
        gate = jnp.where(pick, -jnp.inf, gate)
    return jnp.logical_and(sel, blk < n_valid)


def _moba_prompt_kernel(bias_ref, qbt_ref, kext_ref, vbt_ref, mean_ref, idx_ref, o_ref,
                        qext_scr, tab_scr, b31_scr, m_scr, acc_scr, *, tq, nbp, n_sel):
    i = pl.program_id(1)
    q0 = i * tq
    own = q0 // MOBA_BLOCK
    par = (q0 % MOBA_BLOCK) // tq
    rr = GRP * tq

    @pl.when(jnp.logical_and(pl.program_id(0) == 0, i == 0))
    def _():
        _fill_bias_tables(bias_ref, idx_ref, tab_scr, b31_scr, tq)

    qbt = qbt_ref[...]
    blk = lax.broadcasted_iota(I32, (nbp, rr), 0)
    zeros = jnp.zeros((HEAD_DIM, rr), BF16)
    for g in range(N_KV):
        qg = jnp.concatenate([qbt[(g * GRP + j) * HEAD_DIM:(g * GRP + j + 1) * HEAD_DIM, :] for j in range(GRP)], axis=1)
        mg = mean_ref[:, g * LANES:(g + 1) * LANES].astype(BF16)
        gate = jnp.dot(mg, jnp.concatenate([qg, zeros], axis=0), preferred_element_type=F32)
        ok = jnp.logical_or(_top_blocks(gate, blk, own, n_sel, 0), blk == own)
        qext_scr[g, 0:HEAD_DIM, :] = qg
        qext_scr[g, HEAD_DIM:HEAD_DIM + nbp, :] = jnp.where(ok, 0.0, NEG).astype(BF16)

    _flash_init(m_scr, acc_scr)

    mb = MOBA_BLOCK

    def attend(c0, width, tabs):
        sts = [jnp.dot(kext_ref[g, pl.ds(c0, width), :], qext_scr[g], preferred_element_type=F32)
               for g in range(N_KV)]
        vts = [vbt_ref[(g // 2) * LANES:(g // 2 + 1) * LANES, pl.ds(c0, width)] for g in range(N_KV)]
        if tabs is None:
            _flash_update(sts, [b31_scr[g] for g in range(N_KV)], vts, m_scr, acc_scr)
        else:
            _flash_update([st + t for st, t in zip(sts, tabs)], [None] * N_KV, vts, m_scr, acc_scr)

    nfar = jnp.maximum(own - 1, 0)

    def far2(m, carry):
        attend(pl.multiple_of(m * 2 * mb, 2 * mb), 2 * mb, None)
        return carry

    lax.fori_loop(0, nfar // 2, far2, 0)

    @pl.when(nfar % 2 == 1)
    def _():
        attend(pl.multiple_of((nfar - 1) * mb, mb), mb, None)

    @pl.when(own >= 1)
    def _():
        attend(pl.multiple_of((own - 1) * mb, mb), 2 * mb, [tab_scr[g, par] for g in range(N_KV)])

    @pl.when(own == 0)
    def _():
        attend(0, mb, [tab_scr[g, par, mb:2 * mb] for g in range(N_KV)])

    _flash_finish(o_ref, acc_scr, tq)


def _moba_prompt(qbt, kext, vbt_bf, means, bias_b):
    bx, _, s = qbt.shape
    tq = LANES
    nb = s // MOBA_BLOCK
    nbp = HEAD_DIM
    rr = GRP * tq
    npar = MOBA_BLOCK // tq
    assert s % MOBA_BLOCK == 0 and nb <= nbp and MOBA_BLOCK % tq == 0 and REL_MAX_DIST <= MOBA_BLOCK
    n_sel = min(MOBA_TOPK, nb - 1)
    means = jnp.pad(means, ((0, 0), (0, nbp - nb), (0, 0)))
    c = np.arange(MOBA_BLOCK)[:, None]
    r = np.arange(tq)[None, :]
    slabs = []
    for par in range(npar):
        d_own = par * tq + r - c
        slabs.append(jnp.concatenate([_bucket_index(d_own + MOBA_BLOCK, np.ones_like(d_own, bool)),
                                      _bucket_index(d_own, d_own >= 0)], axis=0))
    idx = jnp.stack(slabs)
    res = lambda shape: pl.BlockSpec((None,) + shape, lambda b, i: (b,) + (0,) * len(shape),
                                     pipeline_mode=pl.Buffered(1))
    return pl.pallas_call(
        functools.partial(_moba_prompt_kernel, tq=tq, nbp=nbp, n_sel=n_sel),
        grid=(bx, s // tq),
        in_specs=[pl.BlockSpec(memory_space=pltpu.SMEM),
                  pl.BlockSpec((None, WIDTH, tq), lambda b, i: (b, 0, i)),
                  res((N_KV, s, LANES)), res((KVW, s)), res((nbp, N_KV * LANES)),
                  pl.BlockSpec(idx.shape, lambda b, i: (0, 0, 0), pipeline_mode=pl.Buffered(1))],
        out_specs=pl.BlockSpec((None, tq, WIDTH), lambda b, i: (b, i, 0)),
        out_shape=jax.ShapeDtypeStruct((bx, s, WIDTH), F32),
        scratch_shapes=[pltpu.VMEM((N_KV, HEAD_DIM + nbp, rr), BF16),
                        pltpu.VMEM((N_KV, npar, 2 * MOBA_BLOCK, rr), F32),
                        pltpu.VMEM((N_KV, 1, rr), F32),
                        pltpu.VMEM((N_KV, 1, rr), F32),
                        pltpu.VMEM((N_KV, LANES + ONES_ROWS, rr), F32)],
        compiler_params=pltpu.CompilerParams(dimension_semantics=("arbitrary", "arbitrary"),
                                             vmem_limit_bytes=VMEM_LIMIT),
        name="moba_prompt",
    )(bias_b.astype(F32), qbt, kext, vbt_bf, means, idx)


def _sample_kernel(*refs, mode, past, page, ds, ck, topk, n_sel):
    if mode == "dsa":
        (pt_ref, qbd_ref, kf_ref, vf_ref, btab_ref, b31_ref, qis_ref, wic_ref, kif_ref,
         ipool, kpool, vpool, o_ref, kbuf, vbuf, ibuf, sem, lg_scr, madd_scr, sc_scr) = refs
        pools, bufs = (kpool, vpool, ipool), (kbuf, vbuf, ibuf)
    else:
        (pt_ref, qbd_ref, kf_ref, vf_ref, btab_ref, b31_ref,
         kpool, vpool, o_ref, kbuf, vbuf, sem, lg_scr, madd_scr, mean_scr) = refs
        pools, bufs = (kpool, vpool), (kbuf, vbuf)
    b = pl.program_id(0)
    nseq = pl.num_programs(0)
    slot = b % 2
    npages = past // page
    nck = past // ck
    lk = past + LANES
    rows = N_HEADS * ds
    nt = (((1,), (1,)), ((), ()))

    def copies(seq, sl):
        out = []
        for a, (pool, buf) in enumerate(zip(pools, bufs)):
            def mk(pg, pool=pool, buf=buf, a=a):
                return pltpu.make_async_copy(pool.at[pt_ref[seq, pg]],
                                             buf.at[sl, :, pl.ds(pl.multiple_of(pg * page, page), page)],
                                             sem.at[a, sl])
            out.append(mk)
        return out

    def start_all(seq, sl):
        def body(pg, c):
            for mk in copies(seq, sl):
                mk(pg).start()
            return c
        lax.fori_loop(0, npages, body, 0)

    def wait_all(seq, sl):
        def body(pg, c):
            for mk in copies(seq, sl):
                mk(pg).wait()
            return c
        lax.fori_loop(0, npages, body, 0)

    @pl.when(b == 0)
    def _():
        start_all(b, slot)

    @pl.when(b + 1 < nseq)
    def _():
        start_all(b + 1, 1 - slot)

    wait_all(b, slot)

    def tail(ref):
        return jnp.concatenate([ref[...], jnp.zeros((LANES - ds, ref.shape[1]), F32)], axis=0).astype(BF16)

    kf_tail = tail(kf_ref)
    vf_tail = tail(vf_ref)
    qrow = lax.broadcasted_iota(I32, (ds, 1), 0)

    if mode == "dsa":
        qis = qis_ref[...]
        wic = wic_ref[...]

        def head_sum(d):
            t = jnp.maximum(d, 0.0) * wic
            return jnp.sum(t.reshape(IDX_HEADS, ds, d.shape[1]), axis=0)

        def sc_body(c, carry):
            r0 = pl.multiple_of(c * ck, ck)
            d = jnp.dot(qis, ibuf[slot, :, pl.ds(r0, ck)].astype(BF16), preferred_element_type=F32)
            sc_scr[:, pl.ds(r0, ck)] = head_sum(d)
            return carry

        lax.fori_loop(0, nck, sc_body, 0)
        lane = lax.broadcasted_iota(I32, (ds, LANES), 1)
        d_tail = lax.dot_general(qis, tail(kif_ref), nt, preferred_element_type=F32)
        sc_scr[:, past:lk] = jnp.where(lane <= qrow, head_sum(d_tail), -jnp.inf)

        x = sc_scr[...]
        rowmax = jnp.max(x, axis=1, keepdims=True)
        small = (past + qrow) < topk

        def count_gt(t):
            return jnp.sum(jnp.where(sc_scr[...] > t, 1.0, 0.0), axis=1, keepdims=True)

        hi, chi = _kth_threshold(count_gt, rowmax, small, topk)
        need = jnp.where(small, 0.0, jnp.float32(topk) - chi)
        any_tie = jnp.max(need) > 0.0

        @pl.when(jnp.logical_not(any_tie))
        def _():
            madd_scr[...] = jnp.where(sc_scr[...] > hi, 0.0, -jnp.inf)

        @pl.when(any_tie)
        def _():
            r = lax.broadcasted_iota(I32, (LANES, LANES), 0)
            c = lax.broadcasted_iota(I32, (LANES, LANES), 1)
            upper = jnp.where(r < c, 1.0, 0.0).astype(BF16)

            def tb(cix, eqb):
                c0 = pl.multiple_of(cix * LANES, LANES)
                xx = sc_scr[:, pl.ds(c0, LANES)]
                eq = xx == hi
                e = jnp.where(eq, 1.0, 0.0)
                rank = jnp.dot(e.astype(BF16), upper, preferred_element_type=F32) + eqb
                sel = jnp.logical_or(xx > hi, jnp.logical_and(eq, rank < need))
                madd_scr[:, pl.ds(c0, LANES)] = jnp.where(sel, 0.0, -jnp.inf)
                return eqb + jnp.sum(e, axis=1, keepdims=True)

            lax.fori_loop(0, lk // LANES, tb, jnp.zeros((ds, 1), F32))

        def madd_at(r0, width):
            m = madd_scr[:, pl.ds(r0, width)]
            return jnp.broadcast_to(m[None], (N_HEADS, ds, width)).reshape(rows, width)
    else:
        nbs = past // MOBA_BLOCK
        for n in range(nbs):
            mean_scr[:, n:n + 1] = jnp.mean(kbuf[slot, :, n * MOBA_BLOCK:(n + 1) * MOBA_BLOCK], axis=1, keepdims=True)
        gate = jnp.dot(qbd_ref[...], mean_scr[...].astype(BF16), preferred_element_type=F32)
        blk = lax.broadcasted_iota(I32, (rows, nbs), 1)
        ok = _top_blocks(gate, blk, nbs, n_sel, 1)
        mv = jnp.where(ok, 0.0, -jnp.inf)
        for n in range(nbs):
            madd_scr[:, n * MOBA_BLOCK:(n + 1) * MOBA_BLOCK] = jnp.broadcast_to(mv[:, n:n + 1], (rows, MOBA_BLOCK))
        madd_scr[:, past:lk] = jnp.zeros((rows, LANES), F32)

        def madd_at(r0, width):
            return madd_scr[:, pl.ds(r0, width)]

    qbd = qbd_ref[...]
    b31 = b31_ref[...]

    def lg_body(c, m):
        r0 = pl.multiple_of(c * ck, ck)
        lg = jnp.dot(qbd, kbuf[slot, :, pl.ds(r0, ck)].astype(BF16), preferred_element_type=F32)
        lg = lg + madd_at(r0, ck) + jnp.where(c == nck - 1, btab_ref[:, 0:ck], b31)
        lg_scr[:, pl.ds(r0, ck)] = lg
        return jnp.maximum(m, jnp.max(lg, axis=1, keepdims=True))

    m = lax.fori_loop(0, nck, lg_body, jnp.full((rows, 1), NEG, F32))
    lgt = (lax.dot_general(qbd, kf_tail, nt, preferred_element_type=F32)
           + madd_at(past, LANES) + btab_ref[:, ck:ck + LANES])
    m = jnp.maximum(m, jnp.max(lgt, axis=1, keepdims=True))

    def pv_body(c, carry):
        l, acc = carry
        r0 = pl.multiple_of(c * ck, ck)
        p = jnp.exp(lg_scr[:, pl.ds(r0, ck)] - m)
        vc = vbuf[slot, :, pl.ds(r0, ck)].astype(BF16)
        return (l + jnp.sum(p, axis=1, keepdims=True),
                acc + lax.dot_general(p.astype(BF16), vc, nt, preferred_element_type=F32))

    l, acc = lax.fori_loop(0, nck, pv_body, (jnp.zeros((rows, 1), F32), jnp.zeros((rows, KVW), F32)))
    pt = jnp.exp(lgt - m)
    l = l + jnp.sum(pt, axis=1, keepdims=True)
    acc = acc + jnp.dot(pt.astype(BF16), vf_tail, preferred_element_type=F32)
    o_ref[...] = acc / l


def _sample_attn(mode, page_table, q_s, k_fresh, v_fresh, k_pool, v_pool, bias, *, topk=0, n_sel=0,
                 qi_s=None, wi_s=None, ki_fresh=None, i_pool=None):
    db, ds, _ = q_s.shape
    npages = page_table.shape[1]
    page = k_pool.shape[2]
    past = npages * page
    ck = min(1024, past)
    rows = N_HEADS * ds
    lk = past + LANES
    assert past % ck == 0 and ck >= REL_MAX_DIST + ds and ds <= LANES and past % MOBA_BLOCK == 0 and ds % 8 == 0

    qh = jnp.transpose(q_s.reshape(db, ds, N_HEADS, HEAD_DIM), (0, 2, 1, 3))
    onehot = jnp.asarray((np.arange(N_HEADS)[:, None] // GRP == np.arange(N_KV)[None, :]).astype(np.float32), BF16)
    qbd = (qh[:, :, :, None, :] * onehot[None, :, None, :, None]).reshape(db, rows, KVW)

    hq = np.arange(ds)[:, None]
    c = np.arange(ck + LANES)[None, :]
    dist = np.where(c < ck, ck + hq - c, hq - (c - ck))
    valid = np.where(c < ck, True, (c - ck) <= hq)
    btab = _bias_table(bias, dist, valid).reshape(rows, ck + LANES)
    b31 = jnp.repeat(bias[REL_BUCKETS - 1].astype(F32), ds).reshape(rows, 1)

    per = lambda shape: pl.BlockSpec((None,) + shape, lambda b, pt: (b, 0, 0))
    const = lambda a: pl.BlockSpec(a.shape, lambda b, pt: (0,) * a.ndim)
    anyspec = pl.BlockSpec(memory_space=pl.ANY)
    in_specs = [per((rows, KVW)), per((ds, KVW)), per((ds, KVW)), const(btab), const(b31)]
    args = [qbd, k_fresh, v_fresh, btab, b31]
    scratch = [pltpu.VMEM((2, KVW, past), F32), pltpu.VMEM((2, KVW, past), F32)]
    if mode == "dsa":
        qis = jnp.transpose(qi_s.reshape(db, ds, IDX_HEADS, IDX_DIM), (0, 2, 1, 3)).reshape(db, IDX_HEADS * ds, IDX_DIM)
        wic = jnp.transpose(wi_s, (0, 2, 1)).reshape(db, IDX_HEADS * ds, 1)
        in_specs += [per((IDX_HEADS * ds, IDX_DIM)), per((IDX_HEADS * ds, 1)), per((ds, IDX_DIM)), anyspec]
        args += [qis, wic, ki_fresh, i_pool]
        scratch += [pltpu.VMEM((2, IDX_DIM, past), F32), pltpu.SemaphoreType.DMA((3, 2)),
                    pltpu.VMEM((rows, lk), F32), pltpu.VMEM((ds, lk), F32), pltpu.VMEM((ds, lk), F32)]
    else:
        scratch += [pltpu.SemaphoreType.DMA((2, 2)),
                    pltpu.VMEM((rows, lk), F32), pltpu.VMEM((rows, lk), F32),
                    pltpu.VMEM((KVW, past // MOBA_BLOCK), F32)]
    in_specs += [anyspec, anyspec]
    args += [k_pool, v_pool]
    o = pl.pallas_call(
        functools.partial(_sample_kernel, mode=mode, past=past, page=page, ds=ds, ck=ck, topk=topk, n_sel=n_sel),
        grid_spec=pltpu.PrefetchScalarGridSpec(
            num_scalar_prefetch=1, grid=(db,), in_specs=in_specs,
            out_specs=pl.BlockSpec((None, rows, KVW), lambda b, pt: (b, 0, 0)),
            scratch_shapes=scratch),
        out_shape=jax.ShapeDtypeStruct((db, rows, KVW), F32),
        compiler_params=pltpu.CompilerParams(dimension_semantics=("arbitrary",), vmem_limit_bytes=VMEM_LIMIT),
        name="sample_" + mode,
    )(page_table, *args)
    o5 = o.reshape(db, N_HEADS, ds, N_KV, HEAD_DIM)
    og = o5[:, np.arange(N_HEADS), :, np.arange(N_HEADS) // GRP, :]
    return jnp.transpose(og, (1, 2, 0, 3)).reshape(db, ds, WIDTH)


def _out_kernel(x_ref, gate_ref, oa_ref, za_ref, ob_ref, zb_ref, wo_ref, fg_ref, y_ref, *, final_norm):
    za = za_ref[...]
    zb = zb_ref[...]
    ma = (oa_ref[...] * (za * jax.nn.sigmoid(za))).astype(BF16)
    mb = (ob_ref[...] * (zb * jax.nn.sigmoid(zb))).astype(BF16)
    r = (jnp.dot(ma, wo_ref[0:WIDTH, :], preferred_element_type=F32)
         + jnp.dot(mb, wo_ref[WIDTH:2 * WIDTH, :], preferred_element_type=F32))
    h = x_ref[...] + gate_ref[...] * r
    if final_norm:
        h = h * lax.rsqrt(jnp.mean(h * h, axis=-1, keepdims=True) + EPS) * fg_ref[...]
    y_ref[...] = h


def _out_call(x, gate, oa, za, ob, zb, w_o, final_g, *, per_row_mod, final_norm):
    bx, s, d = x.shape
    tm = min(512, s)
    assert s % tm == 0
    row = lambda c: pl.BlockSpec((None, tm, c), lambda b, i: (b, i, 0))
    gate_spec = row(d) if per_row_mod else pl.BlockSpec((None, 1, d), lambda b, i: (b, 0, 0))
    return pl.pallas_call(
        functools.partial(_out_kernel, final_norm=final_norm),
        grid=(bx, s // tm),
        in_specs=[row(d), gate_spec, row(WIDTH), row(WIDTH), row(WIDTH), row(WIDTH),
                  pl.BlockSpec(w_o.shape, lambda b, i: (0, 0)), pl.BlockSpec((1, d), lambda b, i: (0, 0))],
        out_specs=row(d),
        out_shape=jax.ShapeDtypeStruct((bx, s, d), F32),
        compiler_params=pltpu.CompilerParams(dimension_semantics=("arbitrary", "arbitrary"),
                                             vmem_limit_bytes=VMEM_LIMIT),
        name="gate_out_residual",
    )(x, gate, oa, za, ob, zb, w_o, final_g.reshape(1, d))


def kernel(x_prompt, x_sample, c_prompt, c_sample, cache_k_a, cache_v_a, cache_idx_k, cache_k_b, cache_v_b,
           page_table, rel_bias, norm_g, w_ada, b_ada, w_in, w_o, final_g):
    bsz, seq, d = x_prompt.shape
    dbsz, dseq, _ = x_sample.shape
    depth = w_in.shape[0]
    n_pool, page = cache_k_a.shape[1], cache_k_a.shape[2]
    past = page_table.shape[1] * page
    bias_a = rel_bias[:, :N_HEADS]
    bias_b = rel_bias[:, N_HEADS:]
    topk_p = min(TOPK_MAX, seq // 4)
    topk_s = min(TOPK_MAX, (past + dseq) // 4)
    nsel_s = min(MOBA_TOPK, past // MOBA_BLOCK)

    hp = x_prompt
    hs = x_sample.reshape(1, dbsz * dseq, d)
    outs = [[] for _ in range(10)]
    for l in range(depth):
        w_l = w_in[l]
        w_cols_p = _prompt_cols(w_l).astype(BF16)
        w_rows_t = jnp.transpose(_pick_cols(w_l, _ROWS_PROMPT_T)).astype(BF16)
        w_cols_s = _pick_cols(w_l, _COLS_SAMPLE).astype(BF16)
        w_o_l = w_o[l].astype(BF16)
        last = l == depth - 1

        mod = _mod_call(jnp.concatenate([c_prompt, c_sample], axis=0), w_ada[l], b_ada[l])
        shift, scale, gate = jnp.split(mod, 3, axis=-1)

        pm = lambda t: t[:bsz].reshape(bsz, 1, d)
        (za, zb, ka_bf, kext, ki_bf, means, qat, qit, qbt, wit, kat, vat, kbt, vbt, kit, vat_bf, vbt_bf) = _proj_call(
            hp, pm(scale), pm(shift), norm_g[l], w_cols_p, w_rows_t, per_row_mod=False)
        oa = _dsa_prompt(qit, wit, qat, ki_bf, ka_bf, vat_bf, bias_a, topk_p)
        ob = _moba_prompt(qbt, kext, vbt_bf, means.reshape(bsz, seq // MOBA_BLOCK, N_KV * LANES), bias_b)
        hp = _out_call(hp, pm(gate), oa, za, ob, zb, w_o_l, final_g, per_row_mod=False, final_norm=last)

        sm = lambda t: jnp.broadcast_to(t[bsz:, None, :], (dbsz, dseq, d)).reshape(1, dbsz * dseq, d)
        (qa_s, qi_s, qb_s, za_s, zb_s, wi_s, ka_s, va_s, kb_s, vb_s, ki_s) = _proj_call(
            hs, sm(scale), sm(shift), norm_g[l], w_cols_s, None, per_row_mod=True)
        sq = lambda t: t.reshape(dbsz, dseq, t.shape[-1])
        pool = lambda c: jnp.moveaxis(c[l], 1, -1).reshape(n_pool, -1, page)
        oa_s = _sample_attn("dsa", page_table, sq(qa_s), sq(ka_s), sq(va_s), pool(cache_k_a), pool(cache_v_a), bias_a,
                            topk=topk_s, qi_s=sq(qi_s), wi_s=sq(wi_s), ki_fresh=sq(ki_s), i_pool=pool(cache_idx_k))
        ob_s = _sample_attn("moba", page_table, sq(qb_s), sq(kb_s), sq(vb_s), pool(cache_k_b), pool(cache_v_b), bias_b,
                            n_sel=nsel_s)
        flat = lambda t: t.reshape(1, dbsz * dseq, WIDTH)
        hs = _out_call(hs, sm(gate), flat(oa_s), za_s, flat(ob_s), zb_s, w_o_l, final_g, per_row_mod=True, final_norm=last)

        kv_t = lambda t: jnp.transpose(t.reshape(bsz, N_KV, HEAD_DIM, seq), (0, 3, 1, 2))
        kv = lambda t: t.reshape(dbsz, dseq, N_KV, HEAD_DIM)
        for lst, val in zip(outs, (kv_t(kat), kv_t(vat), jnp.transpose(kit, (0, 2, 1)), kv_t(kbt), kv_t(vbt),
                                   kv(ka_s), kv(va_s), ki_s.reshape(dbsz, dseq, IDX_DIM), kv(kb_s), kv(vb_s))):
            lst.append(val)

    return (hp, hs.reshape(dbsz, dseq, d)) + tuple(jnp.stack(o) for o in outs)
```

```python
import functools
import math

import numpy as np
import jax
import jax.numpy as jnp
from jax import lax
from jax.experimental import pallas as pl
from jax.experimental.pallas import tpu as pltpu

F32 = jnp.float32
BF16 = jnp.bfloat16
I32 = jnp.int32

HEAD_DIM = 64
N_HEADS = 8
N_KV = 4
GRP = N_HEADS // N_KV
WIDTH = N_HEADS * HEAD_DIM
KVW = N_KV * HEAD_DIM
IDX_HEADS = 8
IDX_DIM = 64
TOPK_MAX = 256
MOBA_BLOCK = 256
MOBA_TOPK = 3
REL_BUCKETS = 32
REL_MAX_DIST = 128
EPS = 1e-6
NEG = -1e30
LANES = 128
VMEM_LIMIT = 56 * 1024 * 1024

_ORIG = (("qa", WIDTH), ("ka", KVW), ("va", KVW), ("za", WIDTH), ("qi", IDX_HEADS * IDX_DIM), ("ki", IDX_DIM),
         ("wi", IDX_HEADS), ("qb", WIDTH), ("kb", KVW), ("vb", KVW), ("zb", WIDTH))
_COLS_SAMPLE = ("qa", "ka", "va", "za", "qi", "qb", "kb", "vb", "zb", "ki", "wi")
_ROWS_PROMPT_T = ("qa", "qi", "qb", "ka", "va", "kb", "vb", "ki", "wi")
_P_ZA, _P_ZB, _P_KA, _P_KEXT, _P_KI, _P_END = 0, WIDTH, 2 * WIDTH, 2 * WIDTH + KVW, 2 * WIDTH + KVW + N_KV * LANES, \
    2 * WIDTH + KVW + N_KV * LANES + LANES


def _col_ranges():
    off, o = {}, 0
    for name, sz in _ORIG:
        off[name] = (o, o + sz)
        o += sz
    return off


def _offsets(order):
    sizes = dict(_ORIG)
    off, o = {}, 0
    for name in order:
        off[name] = (o, o + sizes[name])
        o += sizes[name]
    return off, o


def _pick_cols(w, order):
    cols = _col_ranges()
    parts = [w[:, cols[n][0]:cols[n][1]] for n in order]
    pad = (-sum(p.shape[1] for p in parts)) % LANES
    if pad:
        parts.append(jnp.zeros((w.shape[0], pad), w.dtype))
    return jnp.concatenate(parts, axis=1)


def _prompt_cols(w):
    cols = _col_ranges()
    take = lambda n: w[:, cols[n][0]:cols[n][1]]
    z = jnp.zeros((w.shape[0], HEAD_DIM), w.dtype)
    kb = take("kb")
    parts = [take("za"), take("zb"), take("ka")]
    for g in range(N_KV):
        parts += [kb[:, g * HEAD_DIM:(g + 1) * HEAD_DIM], z]
    parts += [take("ki"), z]
    return jnp.concatenate(parts, axis=1)


def _bucket_np(n):
    n = np.maximum(np.asarray(n, np.int64), 0)
    exact = REL_BUCKETS // 2

    def large(dt):
        nf = np.maximum(n, 1).astype(dt)
        return exact + (np.log(nf / dt(exact)) / dt(math.log(REL_MAX_DIST / exact)) * dt(REL_BUCKETS - exact)).astype(np.int64)

    l32, l64 = large(np.float32), large(np.float64)
    assert np.array_equal(np.minimum(l32, REL_BUCKETS - 1)[n >= exact], np.minimum(l64, REL_BUCKETS - 1)[n >= exact])
    return np.where(n < exact, n, np.minimum(l32, REL_BUCKETS - 1)).astype(np.int32)


def _bucket_index(dist, valid):
    return jnp.asarray(np.where(valid, _bucket_np(dist), -1).astype(np.int32))


def _bias_table(bias, dist, valid):
    t = jnp.transpose(bias.astype(F32)[_bucket_np(dist)], (2, 0, 1))
    return jnp.where(jnp.asarray(valid)[None], t, -jnp.inf)


def _fill_bias_tables(bias_ref, idx_ref, tab_scr, b31_scr, tq):
    lead = idx_ref.shape[:-2]
    for g in range(N_KV):
        for j in range(GRP):
            head = g * GRP + j
            for ix in np.ndindex(*lead):
                idx = idx_ref[ix]

                def pick(k, t, idx=idx, head=head):
                    return jnp.where(idx == k, bias_ref[k, head], t)

                t = lax.fori_loop(0, REL_BUCKETS, pick, jnp.full(idx.shape, -jnp.inf, F32))
                tab_scr[(g,) + ix + (slice(None), slice(j * tq, (j + 1) * tq))] = t
            b31_scr[g, :, j * tq:(j + 1) * tq] = jnp.full((1, tq), bias_ref[REL_BUCKETS - 1, head], F32)


def _mod_kernel(c_ref, w_ref, b_ref, o_ref):
    c = c_ref[...]
    s = c * jax.nn.sigmoid(c)
    o_ref[...] = jnp.dot(s, w_ref[...], preferred_element_type=F32, precision=lax.Precision.HIGHEST) + b_ref[...]


def _mod_call(c, w_ada, b_ada):
    n, d = c.shape
    d3 = w_ada.shape[1]
    tn = 512 if d3 % 512 == 0 else d3
    return pl.pallas_call(
        _mod_kernel,
        grid=(d3 // tn,),
        in_specs=[pl.BlockSpec((n, d), lambda j: (0, 0)),
                  pl.BlockSpec((d, tn), lambda j: (0, j)),
                  pl.BlockSpec((1, tn), lambda j: (0, j))],
        out_specs=pl.BlockSpec((n, tn), lambda j: (0, j)),
        out_shape=jax.ShapeDtypeStruct((n, d3), F32),
        compiler_params=pltpu.CompilerParams(dimension_semantics=("arbitrary",), vmem_limit_bytes=VMEM_LIMIT),
        name="adaln_mod",
    )(c, w_ada, b_ada.reshape(1, d3))


def _proj_kernel(*refs, prompt, tm):
    if prompt:
        (x_ref, sc_ref, sh_ref, g_ref, w_ref, wt_ref,
         za_o, zb_o, kab_o, kext_o, kib_o, mean_o,
         qat_o, qit_o, qbt_o, wit_o, kat_o, vat_o, kbt_o, vbt_o, kit_o, vatb_o, vbtb_o) = refs
    else:
        (x_ref, sc_ref, sh_ref, g_ref, w_ref,
         qa_o, qi_o, qb_o, za_o, zb_o, wi_o, ka_o, va_o, kb_o, vb_o, ki_o) = refs
    x = x_ref[...]
    y = x * lax.rsqrt(jnp.mean(x * x, axis=-1, keepdims=True) + EPS) * g_ref[...]
    h = y * (1.0 + sc_ref[...]) + sh_ref[...]
    hb = h.astype(BF16)
    qscale = HEAD_DIM ** -0.5
    iscale = IDX_DIM ** -0.5
    wscale = IDX_HEADS ** -0.5

    def cols(a, b):
        return jnp.dot(hb, w_ref[:, a:b], preferred_element_type=F32)

    if prompt:
        za_o[...] = cols(_P_ZA, _P_ZB)
        zb_o[...] = cols(_P_ZB, _P_KA)
        kab_o[...] = cols(_P_KA, _P_KEXT).astype(BF16)
        kib_o[...] = cols(_P_KI, _P_END)[:, :IDX_DIM].astype(BF16)
        pos = pl.program_id(1) * tm + lax.broadcasted_iota(I32, (tm, LANES), 0)
        lane = lax.broadcasted_iota(I32, (tm, LANES), 1)
        onehot = jnp.where(lane - HEAD_DIM == jnp.right_shift(pos, int(math.log2(MOBA_BLOCK))), 1.0, 0.0)
        for g in range(N_KV):
            kx = cols(_P_KEXT + g * LANES, _P_KEXT + (g + 1) * LANES)
            kext_o[g] = (kx + onehot).astype(BF16)
            for r in range(tm // MOBA_BLOCK):
                mean_o[r:r + 1, g * LANES:(g + 1) * LANES] = jnp.mean(
                    kx[r * MOBA_BLOCK:(r + 1) * MOBA_BLOCK], axis=0, keepdims=True)
        roff, _ = _offsets(_ROWS_PROMPT_T)
        nt = (((1,), (1,)), ((), ()))

        def rows(name):
            a, b = roff[name]
            return lax.dot_general(wt_ref[a:b, :], hb, nt, preferred_element_type=F32)

        qat_o[...] = (rows("qa") * qscale).astype(BF16)
        qit_o[...] = (rows("qi") * iscale).astype(BF16)
        qbt_o[...] = (rows("qb") * qscale).astype(BF16)
        a = roff["wi"][0]
        wit_o[...] = lax.dot_general(wt_ref[a:a + 16, :], hb, nt, preferred_element_type=F32)[:IDX_HEADS] * wscale
        kat_o[...] = rows("ka")
        kbt_o[...] = rows("kb")
        kit_o[...] = rows("ki")
        vat = rows("va")
        vbt = rows("vb")
        vat_o[...] = vat
        vbt_o[...] = vbt
        vatb_o[...] = vat.astype(BF16)
        vbtb_o[...] = vbt.astype(BF16)
    else:
        off, _ = _offsets(_COLS_SAMPLE)
        seg = lambda name: cols(*off[name])
        qa_o[...] = (seg("qa") * qscale).astype(BF16)
        qi_o[...] = (seg("qi") * iscale).astype(BF16)
        qb_o[...] = (seg("qb") * qscale).astype(BF16)
        za_o[...] = seg("za")
        zb_o[...] = seg("zb")
        ka_o[...] = seg("ka")
        va_o[...] = seg("va")
        kb_o[...] = seg("kb")
        vb_o[...] = seg("vb")
        a, _ = off["ki"]
        kiwi = cols(a, a + LANES)
        ki_o[...] = kiwi[:, :IDX_DIM]
        wi_o[...] = kiwi[:, IDX_DIM:IDX_DIM + IDX_HEADS] * wscale


def _proj_call(x, scale, shift, g, w_cols, w_rows_t, *, per_row_mod):
    bx, s, d = x.shape
    tm = min(512, s)
    assert s % tm == 0 and tm % 8 == 0
    prompt = w_rows_t is not None
    nw = w_cols.shape[1]
    row = lambda c: pl.BlockSpec((None, tm, c), lambda b, i: (b, i, 0))
    mod_spec = row(d) if per_row_mod else pl.BlockSpec((None, 1, d), lambda b, i: (b, 0, 0))
    in_specs = [row(d), mod_spec, mod_spec,
                pl.BlockSpec((1, d), lambda b, i: (0, 0)),
                pl.BlockSpec((d, nw), lambda b, i: (0, 0))]
    args = [x, scale, shift, g.reshape(1, d), w_cols]
    sds = lambda c, dt: jax.ShapeDtypeStruct((bx, s, c), dt)
    if prompt:
        assert tm % MOBA_BLOCK == 0 and MOBA_BLOCK == 256 and nw == _P_END
        in_specs.append(pl.BlockSpec(w_rows_t.shape, lambda b, i: (0, 0)))
        args.append(w_rows_t)
        col_t = lambda r: pl.BlockSpec((None, r, tm), lambda b, i: (b, 0, i))
        sds_t = lambda r, dt: jax.ShapeDtypeStruct((bx, r, s), dt)
        nblk = tm // MOBA_BLOCK
        out_shape = [sds(WIDTH, F32), sds(WIDTH, F32), sds(KVW, BF16),
                     jax.ShapeDtypeStruct((bx, N_KV, s, LANES), BF16), sds(IDX_DIM, BF16),
                     jax.ShapeDtypeStruct((bx, s // tm, nblk, N_KV * LANES), F32),
                     sds_t(WIDTH, BF16), sds_t(WIDTH, BF16), sds_t(WIDTH, BF16), sds_t(IDX_HEADS, F32),
                     sds_t(KVW, F32), sds_t(KVW, F32), sds_t(KVW, F32), sds_t(KVW, F32), sds_t(IDX_DIM, F32),
                     sds_t(KVW, BF16), sds_t(KVW, BF16)]
        out_specs = [row(WIDTH), row(WIDTH), row(KVW),
                     pl.BlockSpec((None, N_KV, tm, LANES), lambda b, i: (b, 0, i, 0)), row(IDX_DIM),
                     pl.BlockSpec((None, None, nblk, N_KV * LANES), lambda b, i: (b, i, 0, 0)),
                     col_t(WIDTH), col_t(WIDTH), col_t(WIDTH), col_t(IDX_HEADS),
                     col_t(KVW), col_t(KVW), col_t(KVW), col_t(KVW), col_t(IDX_DIM),
                     col_t(KVW), col_t(KVW)]
    else:
        out_shape = [sds(WIDTH, BF16), sds(WIDTH, BF16), sds(WIDTH, BF16), sds(WIDTH, F32), sds(WIDTH, F32),
                     sds(IDX_HEADS, F32), sds(KVW, F32), sds(KVW, F32), sds(KVW, F32), sds(KVW, F32), sds(IDX_DIM, F32)]
        out_specs = [row(WIDTH)] * 5 + [row(IDX_HEADS)] + [row(KVW)] * 4 + [row(IDX_DIM)]
    return pl.pallas_call(
        functools.partial(_proj_kernel, prompt=prompt, tm=tm),
        grid=(bx, s // tm),
        in_specs=in_specs, out_specs=out_specs, out_shape=out_shape,
        compiler_params=pltpu.CompilerParams(dimension_semantics=("arbitrary", "arbitrary"),
                                             vmem_limit_bytes=VMEM_LIMIT),
        name="norm_mod_proj",
    )(*args)


ONES_ROWS = 16


def _flash_init(m_scr, acc_scr):
    m_scr[...] = jnp.full(m_scr.shape, NEG, F32)
    acc_scr[...] = jnp.zeros(acc_scr.shape, F32)


def _flash_update(sts, shifts, vts, m_scr, acc_scr):
    ps, alphas = [], []
    for g, (st, shift) in enumerate(zip(sts, shifts)):
        m_old = m_scr[g]
        mb = jnp.max(st, axis=0, keepdims=True)
        if shift is not None:
            mb = mb + shift
        m_new = jnp.maximum(m_old, mb)
        m_scr[g] = m_new
        alphas.append(jnp.exp(m_old - m_new))
        ps.append(jnp.exp((st - (m_new if shift is None else m_new - shift)).astype(BF16)))
    for g, (p, alpha, vt) in enumerate(zip(ps, alphas, vts)):
        vt1 = jnp.concatenate([vt, jnp.ones((ONES_ROWS, vt.shape[1]), BF16)], axis=0)
        acc_scr[g] = alpha * acc_scr[g] + jnp.dot(vt1, p, preferred_element_type=F32)


def _flash_finish(o_ref, acc_scr, tq):
    for g in range(N_KV):
        acc = acc_scr[g]
        o = jnp.transpose(acc[:LANES] / acc[LANES:LANES + 1])
        c0 = (g % 2) * HEAD_DIM
        for j in range(GRP):
            hd = g * GRP + j
            o_ref[:, hd * HEAD_DIM:(hd + 1) * HEAD_DIM] = o[j * tq:(j + 1) * tq, c0:c0 + HEAD_DIM]


def _f2key(x):
    b = lax.bitcast_convert_type(x, I32)
    return b ^ ((b >> 31) & jnp.int32(0x7FFFFFFF))


def _key2f(k):
    return lax.bitcast_convert_type(k ^ ((k >> 31) & jnp.int32(0x7FFFFFFF)), F32)


def _kth_threshold(count_gt, vmax, small, k, nvis):
    kf = jnp.float32(k)
    logk = jnp.float32(math.log(k))
    lo0 = jnp.full(vmax.shape, _f2key(jnp.float32(-jnp.inf)), I32)
    hi0 = jnp.where(small, lo0, _f2key(vmax))
    st0 = (jnp.int32(0), lo0, hi0, nvis.astype(F32), jnp.zeros(vmax.shape, F32), small.astype(I32))

    def cond(st):
        return jnp.logical_and(st[0] < 160, jnp.min(st[5]) == 0)

    def body(st):
        it, lo, hi, clo, chi, done = st
        mid_key = (lo & hi) + ((lo ^ hi) >> 1)
        conv = mid_key == lo
        lof = _key2f(lo)
        hif = _key2f(hi)
        llo = jnp.log(clo)
        w = (llo - logk) / (llo - jnp.log(jnp.maximum(chi, 0.5)))
        phase = lax.rem(it, 4)
        probe = jnp.where(phase == 1, 0.5 * lof + 0.5 * hif, lof + w * (hif - lof))
        pk = _f2key(probe)
        inside = jnp.logical_and(jnp.logical_and(pk > lo, pk < hi), lof > -jnp.inf)
        off_zero = jnp.logical_or(lof > 0.0, hif < 0.0)
        use = jnp.logical_and(jnp.logical_and(inside, off_zero), phase != 3)
        mid = jnp.where(use, pk, mid_key)
        cnt = count_gt(_key2f(mid))
        active = jnp.logical_and(done == 0, jnp.logical_not(conv))
        up = jnp.logical_and(active, cnt > kf)
        down = jnp.logical_and(active, cnt <= kf)
        lo = jnp.where(up, mid, lo)
        clo = jnp.where(up, cnt, clo)
        hi = jnp.where(down, mid, hi)
        chi = jnp.where(down, cnt, chi)
        fin = jnp.logical_or(conv, jnp.logical_and(active, cnt == kf))
        return it + 1, lo, hi, clo, chi, jnp.where(fin, 1, done)

    _, _, hi, _, chi, _ = lax.while_loop(cond, body, st0)
    return _key2f(hi), chi


def _dsa_prompt_kernel(bias_ref, qit_ref, wit_ref, qat_ref, ki_ref, ka_ref, vat_ref, idx_ref, o_ref,
                       s_scr, qis_scr, qext_scr, tab_scr, b31_scr, m_scr, acc_scr, *, tq, fw, topk):
    i = pl.program_id(1)
    q0 = i * tq
    rr = GRP * tq

    @pl.when(jnp.logical_and(pl.program_id(0) == 0, i == 0))
    def _():
        _fill_bias_tables(bias_ref, idx_ref, tab_scr, b31_scr, tq)

    qit = qit_ref[...]
    qat = qat_ref[...]
    wit = wit_ref[...]
    for h in range(IDX_HEADS):
        qis_scr[:, h * tq:(h + 1) * tq] = qit[h * IDX_DIM:(h + 1) * IDX_DIM, :]
    for g in range(N_KV):
        r0 = (g % 2) * HEAD_DIM
        qext_scr[g, HEAD_DIM - r0:2 * HEAD_DIM - r0, :] = jnp.zeros((HEAD_DIM, rr), BF16)
        for j in range(GRP):
            hd = g * GRP + j
            qext_scr[g, r0:r0 + HEAD_DIM, j * tq:(j + 1) * tq] = qat[hd * HEAD_DIM:(hd + 1) * HEAD_DIM, :]

    qpos = q0 + lax.broadcasted_iota(I32, (1, tq), 1)

    nb1 = (q0 + tq + fw - 1) // fw

    def p1(jb, vmax):
        k0 = pl.multiple_of(jb * fw, fw)
        d = jnp.dot(ki_ref[pl.ds(k0, fw), :], qis_scr[...], preferred_element_type=F32)
        acc = jnp.maximum(d[:, 0:tq], 0.0) * wit[0:1, :]
        for h in range(1, IDX_HEADS):
            acc = acc + jnp.maximum(d[:, h * tq:(h + 1) * tq], 0.0) * wit[h:h + 1, :]
        key = k0 + lax.broadcasted_iota(I32, (fw, tq), 0)
        sc = jnp.where(key <= qpos, acc, -jnp.inf)
        s_scr[pl.ds(k0, fw), :] = sc
        return jnp.maximum(vmax, jnp.max(sc, axis=0, keepdims=True))

    vmax = lax.fori_loop(0, nb1, p1, jnp.full((1, tq), -jnp.inf, F32))

    def count_gt(t):
        def cb(jb, c):
            x = s_scr[pl.ds(pl.multiple_of(jb * fw, fw), fw), :]
            one = jnp.where(x > t, 1.0, 0.0)
            return c + jnp.sum(one.reshape(fw // 64, 64, tq), axis=0)
        c = lax.fori_loop(0, nb1, cb, jnp.zeros((64, tq), F32))
        return jnp.sum(c, axis=0, keepdims=True)

    small = qpos < topk
    hi, chi = _kth_threshold(count_gt, vmax, small, topk, qpos + 1)
    need = jnp.where(small, 0.0, jnp.float32(topk) - chi)
    any_tie = jnp.max(need) > 0.0

    _flash_init(m_scr, acc_scr)
    nfar = jnp.maximum(q0 - LANES, 0) // fw
    near0 = nfar * (fw // LANES)

    def select(x, eqb, with_ties, width):
        gt = x > hi
        if not with_ties:
            return gt, eqb
        eq = x == hi
        e = jnp.where(eq, 1.0, 0.0)
        r = lax.broadcasted_iota(I32, (width, width), 0)
        c = lax.broadcasted_iota(I32, (width, width), 1)
        lower = jnp.where(c < r, 1.0, 0.0).astype(BF16)
        rank = jnp.dot(lower, e.astype(BF16), preferred_element_type=F32) + eqb
        sel = jnp.logical_or(gt, jnp.logical_and(eq, rank < need))
        return sel, eqb + jnp.sum(e, axis=0, keepdims=True)

    def attend(with_ties):
        def masked_qk(k0, width, eqb):
            sel, eqb = select(s_scr[pl.ds(k0, width), :], eqb, with_ties, width)
            madd = jnp.where(sel, 0.0, -jnp.inf)
            madd = jnp.concatenate([madd] * GRP, axis=1)
            sts = tuple(jnp.dot(ka_ref[pl.ds(k0, width), (g // 2) * LANES:(g // 2 + 1) * LANES], qext_scr[g],
                                preferred_element_type=F32) + madd for g in range(N_KV))
            return sts, eqb

        def softmax_pv(k0, width, sts, t):
            vts = [vat_ref[(g // 2) * LANES:(g // 2 + 1) * LANES, pl.ds(k0, width)] for g in range(N_KV)]
            if t is None:
                _flash_update(sts, [b31_scr[g] for g in range(N_KV)], vts, m_scr, acc_scr)
            else:
                _flash_update([st + tab_scr[g, t] for g, st in enumerate(sts)], [None] * N_KV, vts,
                              m_scr, acc_scr)

        def far(jb, eqb):
            k0 = pl.multiple_of(jb * fw, fw)
            sts, eqb = masked_qk(k0, fw, eqb)
            softmax_pv(k0, fw, sts, None)
            return eqb

        def near(jj, eqb):
            k0 = pl.multiple_of(jj * LANES, LANES)
            sts, eqb = masked_qk(k0, LANES, eqb)
            softmax_pv(k0, LANES, sts, jnp.clip(jj - i + 2, 0, 2))
            return eqb

        eqb = lax.fori_loop(0, nfar, far, jnp.zeros((1, tq), F32))
        lax.fori_loop(near0, i + 1, near, eqb)

    @pl.when(any_tie)
    def _():
        attend(True)

    @pl.when(jnp.logical_not(any_tie))
    def _():
        attend(False)

    _flash_finish(o_ref, acc_scr, tq)


def _dsa_prompt(qit, wit, qat, ki_bf, ka_bf, vat_bf, bias_a, topk):
    bx, _, s = qat.shape
    tq = LANES
    fw = min(512, s)
    rr = GRP * tq
    assert s % fw == 0 and s % tq == 0 and REL_MAX_DIST <= LANES
    c = np.arange(LANES)[:, None]
    r = np.arange(tq)[None, :]
    ones = np.ones((LANES, tq), bool)
    idx = jnp.stack([_bucket_index(np.full((LANES, tq), REL_MAX_DIST), ones),
                     _bucket_index(r - c + LANES, ones),
                     _bucket_index(r - c, (r - c) >= 0)])
    res = lambda shape: pl.BlockSpec((None,) + shape, lambda b, i: (b, 0, 0), pipeline_mode=pl.Buffered(1))
    blk_t = lambda rows: pl.BlockSpec((None, rows, tq), lambda b, i: (b, 0, i))
    return pl.pallas_call(
        functools.partial(_dsa_prompt_kernel, tq=tq, fw=fw, topk=topk),
        grid=(bx, s // tq),
        in_specs=[pl.BlockSpec(memory_space=pltpu.SMEM),
                  blk_t(IDX_HEADS * IDX_DIM), blk_t(IDX_HEADS), blk_t(WIDTH),
                  res((s, IDX_DIM)), res((s, KVW)), res((KVW, s)),
                  pl.BlockSpec(idx.shape, lambda b, i: (0, 0, 0), pipeline_mode=pl.Buffered(1))],
        out_specs=pl.BlockSpec((None, tq, WIDTH), lambda b, i: (b, i, 0)),
        out_shape=jax.ShapeDtypeStruct((bx, s, WIDTH), F32),
        scratch_shapes=[pltpu.VMEM((s, tq), F32),
                        pltpu.VMEM((IDX_DIM, IDX_HEADS * tq), BF16),
                        pltpu.VMEM((N_KV, LANES, rr), BF16),
                        pltpu.VMEM((N_KV, 3, LANES, rr), F32),
                        pltpu.VMEM((N_KV, 1, rr), F32),
                        pltpu.VMEM((N_KV, 1, rr), F32),
                        pltpu.VMEM((N_KV, LANES + ONES_ROWS, rr), F32)],
        compiler_params=pltpu.CompilerParams(dimension_semantics=("arbitrary", "arbitrary"),
                                             vmem_limit_bytes=VMEM_LIMIT),
        name="dsa_prompt",
    )(bias_a.astype(F32), qit, wit, qat, ki_bf, ka_bf, vat_bf, idx)


def _top_blocks(gate, blk, n_valid, n_sel, axis):
    nb = gate.shape[axis]
    gate = jnp.where(blk < n_valid, gate, -jnp.inf)
    sel = jnp.zeros(gate.shape, jnp.bool_)
    for _ in range(n_sel):
        mx = jnp.max(gate, axis=axis, keepdims=True)
        first = jnp.min(jnp.where(gate == mx, blk, nb), axis=axis, keepdims=True)
        pick = blk == first
        sel = jnp.logical_or(sel, pick)
        gate = jnp.where(pick, -jnp.inf, gate)
    return jnp.logical_and(sel, blk < n_valid)


def _moba_prompt_kernel(bias_ref, qbt_ref, kext_ref, vbt_ref, mean_ref, idx_ref, o_ref,
                        qext_scr, tab_scr, b31_scr, m_scr, acc_scr, *, tq, nbp, n_sel):
    i = pl.program_id(1)
    q0 = i * tq
    own = q0 // MOBA_BLOCK
    par = (q0 % MOBA_BLOCK) // tq
    rr = GRP * tq

    @pl.when(jnp.logical_and(pl.program_id(0) == 0, i == 0))
    def _():
        _fill_bias_tables(bias_ref, idx_ref, tab_scr, b31_scr, tq)

    qbt = qbt_ref[...]
    blk = lax.broadcasted_iota(I32, (nbp, rr), 0)
    zeros = jnp.zeros((HEAD_DIM, rr), BF16)
    for g in range(N_KV):
        qg = jnp.concatenate([qbt[(g * GRP + j) * HEAD_DIM:(g * GRP + j + 1) * HEAD_DIM, :] for j in range(GRP)], axis=1)
        mg = mean_ref[:, g * LANES:(g + 1) * LANES].astype(BF16)
        gate = jnp.dot(mg, jnp.concatenate([qg, zeros], axis=0), preferred_element_type=F32)
        ok = jnp.logical_or(_top_blocks(gate, blk, own, n_sel, 0), blk == own)
        qext_scr[g, 0:HEAD_DIM, :] = qg
        qext_scr[g, HEAD_DIM:HEAD_DIM + nbp, :] = jnp.where(ok, 0.0, NEG).astype(BF16)

    _flash_init(m_scr, acc_scr)

    mb = MOBA_BLOCK

    def attend(c0, width, tabs):
        sts = [jnp.dot(kext_ref[g, pl.ds(c0, width), :], qext_scr[g], preferred_element_type=F32)
               for g in range(N_KV)]
        vts = [vbt_ref[(g // 2) * LANES:(g // 2 + 1) * LANES, pl.ds(c0, width)] for g in range(N_KV)]
        if tabs is None:
            _flash_update(sts, [b31_scr[g] for g in range(N_KV)], vts, m_scr, acc_scr)
        else:
            _flash_update([st + t for st, t in zip(sts, tabs)], [None] * N_KV, vts, m_scr, acc_scr)

    nfar = jnp.maximum(own - 1, 0)

    def far4(m, carry):
        c0 = pl.multiple_of(m * 4 * mb, 4 * mb)
        attend(c0, 2 * mb, None)
        attend(c0 + 2 * mb, 2 * mb, None)
        return carry

    lax.fori_loop(0, nfar // 4, far4, 0)

    @pl.when((nfar // 2) % 2 == 1)
    def _():
        attend(pl.multiple_of((nfar // 4) * 4 * mb, 2 * mb), 2 * mb, None)

    @pl.when(nfar % 2 == 1)
    def _():
        attend(pl.multiple_of((nfar - 1) * mb, mb), mb, None)

    @pl.when(own >= 1)
    def _():
        attend(pl.multiple_of((own - 1) * mb, mb), 2 * mb, [tab_scr[g, par] for g in range(N_KV)])

    @pl.when(own == 0)
    def _():
        attend(0, mb, [tab_scr[g, par, mb:2 * mb] for g in range(N_KV)])

    _flash_finish(o_ref, acc_scr, tq)


def _moba_prompt(qbt, kext, vbt_bf, means, bias_b):
    bx, _, s = qbt.shape
    tq = LANES
    nb = s // MOBA_BLOCK
    nbp = HEAD_DIM
    rr = GRP * tq
    npar = MOBA_BLOCK // tq
    assert s % MOBA_BLOCK == 0 and nb <= nbp and MOBA_BLOCK % tq == 0 and REL_MAX_DIST <= MOBA_BLOCK
    n_sel = min(MOBA_TOPK, nb - 1)
    means = jnp.pad(means, ((0, 0), (0, nbp - nb), (0, 0)))
    c = np.arange(MOBA_BLOCK)[:, None]
    r = np.arange(tq)[None, :]
    slabs = []
    for par in range(npar):
        d_own = par * tq + r - c
        slabs.append(jnp.concatenate([_bucket_index(d_own + MOBA_BLOCK, np.ones_like(d_own, bool)),
                                      _bucket_index(d_own, d_own >= 0)], axis=0))
    idx = jnp.stack(slabs)
    res = lambda shape: pl.BlockSpec((None,) + shape, lambda b, i: (b,) + (0,) * len(shape),
                                     pipeline_mode=pl.Buffered(1))
    return pl.pallas_call(
        functools.partial(_moba_prompt_kernel, tq=tq, nbp=nbp, n_sel=n_sel),
        grid=(bx, s // tq),
        in_specs=[pl.BlockSpec(memory_space=pltpu.SMEM),
                  pl.BlockSpec((None, WIDTH, tq), lambda b, i: (b, 0, i)),
                  res((N_KV, s, LANES)), res((KVW, s)), res((nbp, N_KV * LANES)),
                  pl.BlockSpec(idx.shape, lambda b, i: (0, 0, 0), pipeline_mode=pl.Buffered(1))],
        out_specs=pl.BlockSpec((None, tq, WIDTH), lambda b, i: (b, i, 0)),
        out_shape=jax.ShapeDtypeStruct((bx, s, WIDTH), F32),
        scratch_shapes=[pltpu.VMEM((N_KV, HEAD_DIM + nbp, rr), BF16),
                        pltpu.VMEM((N_KV, npar, 2 * MOBA_BLOCK, rr), F32),
                        pltpu.VMEM((N_KV, 1, rr), F32),
                        pltpu.VMEM((N_KV, 1, rr), F32),
                        pltpu.VMEM((N_KV, LANES + ONES_ROWS, rr), F32)],
        compiler_params=pltpu.CompilerParams(dimension_semantics=("arbitrary", "arbitrary"),
                                             vmem_limit_bytes=VMEM_LIMIT),
        name="moba_prompt",
    )(bias_b.astype(F32), qbt, kext, vbt_bf, means, idx)


def _sample_kernel(*refs, mode, past, page, ds, ck, topk, n_sel):
    if mode == "dsa":
        (pt_ref, qbd_ref, kf_ref, vf_ref, btab_ref, b31_ref, qis_ref, wic_ref, kif_ref,
         ipool, kpool, vpool, o_ref, kbuf, vbuf, ibuf, sem, lg_scr, madd_scr, sc_scr) = refs
        pools, bufs = (kpool, vpool, ipool), (kbuf, vbuf, ibuf)
    else:
        (pt_ref, qbd_ref, kf_ref, vf_ref, btab_ref, b31_ref,
         kpool, vpool, o_ref, kbuf, vbuf, sem, lg_scr, madd_scr, mean_scr) = refs
        pools, bufs = (kpool, vpool), (kbuf, vbuf)
    b = pl.program_id(0)
    nseq = pl.num_programs(0)
    slot = b % 2
    npages = past // page
    nck = past // ck
    lk = past + LANES
    rows = N_HEADS * ds
    nt = (((1,), (1,)), ((), ()))

    def copies(seq, sl):
        out = []
        for a, (pool, buf) in enumerate(zip(pools, bufs)):
            def mk(pg, pool=pool, buf=buf, a=a):
                return pltpu.make_async_copy(pool.at[pt_ref[seq, pg]],
                                             buf.at[sl, :, pl.ds(pl.multiple_of(pg * page, page), page)],
                                             sem.at[a, sl])
            out.append(mk)
        return out

    def start_all(seq, sl):
        def body(pg, c):
            for mk in copies(seq, sl):
                mk(pg).start()
            return c
        lax.fori_loop(0, npages, body, 0)

    def wait_all(seq, sl):
        def body(pg, c):
            for mk in copies(seq, sl):
                mk(pg).wait()
            return c
        lax.fori_loop(0, npages, body, 0)

    @pl.when(b == 0)
    def _():
        start_all(b, slot)

    @pl.when(b + 1 < nseq)
    def _():
        start_all(b + 1, 1 - slot)

    wait_all(b, slot)

    def tail(ref):
        return jnp.concatenate([ref[...], jnp.zeros((LANES - ds, ref.shape[1]), F32)], axis=0).astype(BF16)

    kf_tail = tail(kf_ref)
    vf_tail = tail(vf_ref)
    qrow = lax.broadcasted_iota(I32, (ds, 1), 0)

    if mode == "dsa":
        qis = qis_ref[...]
        wic = wic_ref[...]

        def head_sum(d):
            t = jnp.maximum(d, 0.0) * wic
            return jnp.sum(t.reshape(IDX_HEADS, ds, d.shape[1]), axis=0)

        def sc_body(c, carry):
            r0 = pl.multiple_of(c * ck, ck)
            d = jnp.dot(qis, ibuf[slot, :, pl.ds(r0, ck)].astype(BF16), preferred_element_type=F32)
            sc_scr[:, pl.ds(r0, ck)] = head_sum(d)
            return carry

        lax.fori_loop(0, nck, sc_body, 0)
        lane = lax.broadcasted_iota(I32, (ds, LANES), 1)
        d_tail = lax.dot_general(qis, tail(kif_ref), nt, preferred_element_type=F32)
        sc_scr[:, past:lk] = jnp.where(lane <= qrow, head_sum(d_tail), -jnp.inf)

        x = sc_scr[...]
        rowmax = jnp.max(x, axis=1, keepdims=True)
        small = (past + qrow) < topk

        def count_gt(t):
            return jnp.sum(jnp.where(sc_scr[...] > t, 1.0, 0.0), axis=1, keepdims=True)

        hi, chi = _kth_threshold(count_gt, rowmax, small, topk, past + qrow + 1)
        need = jnp.where(small, 0.0, jnp.float32(topk) - chi)
        any_tie = jnp.max(need) > 0.0

        @pl.when(jnp.logical_not(any_tie))
        def _():
            madd_scr[...] = jnp.where(sc_scr[...] > hi, 0.0, -jnp.inf)

        @pl.when(any_tie)
        def _():
            r = lax.broadcasted_iota(I32, (LANES, LANES), 0)
            c = lax.broadcasted_iota(I32, (LANES, LANES), 1)
            upper = jnp.where(r < c, 1.0, 0.0).astype(BF16)

            def tb(cix, eqb):
                c0 = pl.multiple_of(cix * LANES, LANES)
                xx = sc_scr[:, pl.ds(c0, LANES)]
                eq = xx == hi
                e = jnp.where(eq, 1.0, 0.0)
                rank = jnp.dot(e.astype(BF16), upper, preferred_element_type=F32) + eqb
                sel = jnp.logical_or(xx > hi, jnp.logical_and(eq, rank < need))
                madd_scr[:, pl.ds(c0, LANES)] = jnp.where(sel, 0.0, -jnp.inf)
                return eqb + jnp.sum(e, axis=1, keepdims=True)

            lax.fori_loop(0, lk // LANES, tb, jnp.zeros((ds, 1), F32))

        def madd_at(r0, width):
            m = madd_scr[:, pl.ds(r0, width)]
            return jnp.broadcast_to(m[None], (N_HEADS, ds, width)).reshape(rows, width)
    else:
        nbs = past // MOBA_BLOCK
        for n in range(nbs):
            mean_scr[:, n:n + 1] = jnp.mean(kbuf[slot, :, n * MOBA_BLOCK:(n + 1) * MOBA_BLOCK], axis=1, keepdims=True)
        gate = jnp.dot(qbd_ref[...], mean_scr[...].astype(BF16), preferred_element_type=F32)
        blk = lax.broadcasted_iota(I32, (rows, nbs), 1)
        ok = _top_blocks(gate, blk, nbs, n_sel, 1)
        mv = jnp.where(ok, 0.0, -jnp.inf)
        for n in range(nbs):
            madd_scr[:, n * MOBA_BLOCK:(n + 1) * MOBA_BLOCK] = jnp.broadcast_to(mv[:, n:n + 1], (rows, MOBA_BLOCK))
        madd_scr[:, past:lk] = jnp.zeros((rows, LANES), F32)

        def madd_at(r0, width):
            return madd_scr[:, pl.ds(r0, width)]

    qbd = qbd_ref[...]
    b31 = b31_ref[...]

    def lg_body(c, m):
        r0 = pl.multiple_of(c * ck, ck)
        lg = jnp.dot(qbd, kbuf[slot, :, pl.ds(r0, ck)].astype(BF16), preferred_element_type=F32)
        lg = lg + madd_at(r0, ck) + jnp.where(c == nck - 1, btab_ref[:, 0:ck], b31)
        lg_scr[:, pl.ds(r0, ck)] = lg
        return jnp.maximum(m, jnp.max(lg, axis=1, keepdims=True))

    m = lax.fori_loop(0, nck, lg_body, jnp.full((rows, 1), NEG, F32))
    lgt = (lax.dot_general(qbd, kf_tail, nt, preferred_element_type=F32)
           + madd_at(past, LANES) + btab_ref[:, ck:ck + LANES])
    m = jnp.maximum(m, jnp.max(lgt, axis=1, keepdims=True))

    def pv_body(c, carry):
        l, acc = carry
        r0 = pl.multiple_of(c * ck, ck)
        p = jnp.exp(lg_scr[:, pl.ds(r0, ck)] - m)
        vc = vbuf[slot, :, pl.ds(r0, ck)].astype(BF16)
        return (l + jnp.sum(p, axis=1, keepdims=True),
                acc + lax.dot_general(p.astype(BF16), vc, nt, preferred_element_type=F32))

    l, acc = lax.fori_loop(0, nck, pv_body, (jnp.zeros((rows, 1), F32), jnp.zeros((rows, KVW), F32)))
    pt = jnp.exp(lgt - m)
    l = l + jnp.sum(pt, axis=1, keepdims=True)
    acc = acc + jnp.dot(pt.astype(BF16), vf_tail, preferred_element_type=F32)
    o_ref[...] = acc / l


def _sample_attn(mode, page_table, q_s, k_fresh, v_fresh, k_pool, v_pool, bias, *, topk=0, n_sel=0,
                 qi_s=None, wi_s=None, ki_fresh=None, i_pool=None):
    db, ds, _ = q_s.shape
    npages = page_table.shape[1]
    page = k_pool.shape[2]
    past = npages * page
    ck = min(1024, past)
    rows = N_HEADS * ds
    lk = past + LANES
    assert past % ck == 0 and ck >= REL_MAX_DIST + ds and ds <= LANES and past % MOBA_BLOCK == 0 and ds % 8 == 0

    qh = jnp.transpose(q_s.reshape(db, ds, N_HEADS, HEAD_DIM), (0, 2, 1, 3))
    onehot = jnp.asarray((np.arange(N_HEADS)[:, None] // GRP == np.arange(N_KV)[None, :]).astype(np.float32), BF16)
    qbd = (qh[:, :, :, None, :] * onehot[None, :, None, :, None]).reshape(db, rows, KVW)

    hq = np.arange(ds)[:, None]
    c = np.arange(ck + LANES)[None, :]
    dist = np.where(c < ck, ck + hq - c, hq - (c - ck))
    valid = np.where(c < ck, True, (c - ck) <= hq)
    btab = _bias_table(bias, dist, valid).reshape(rows, ck + LANES)
    b31 = jnp.repeat(bias[REL_BUCKETS - 1].astype(F32), ds).reshape(rows, 1)

    per = lambda shape: pl.BlockSpec((None,) + shape, lambda b, pt: (b, 0, 0))
    const = lambda a: pl.BlockSpec(a.shape, lambda b, pt: (0,) * a.ndim)
    anyspec = pl.BlockSpec(memory_space=pl.ANY)
    in_specs = [per((rows, KVW)), per((ds, KVW)), per((ds, KVW)), const(btab), const(b31)]
    args = [qbd, k_fresh, v_fresh, btab, b31]
    scratch = [pltpu.VMEM((2, KVW, past), F32), pltpu.VMEM((2, KVW, past), F32)]
    if mode == "dsa":
        qis = jnp.transpose(qi_s.reshape(db, ds, IDX_HEADS, IDX_DIM), (0, 2, 1, 3)).reshape(db, IDX_HEADS * ds, IDX_DIM)
        wic = jnp.transpose(wi_s, (0, 2, 1)).reshape(db, IDX_HEADS * ds, 1)
        in_specs += [per((IDX_HEADS * ds, IDX_DIM)), per((IDX_HEADS * ds, 1)), per((ds, IDX_DIM)), anyspec]
        args += [qis, wic, ki_fresh, i_pool]
        scratch += [pltpu.VMEM((2, IDX_DIM, past), F32), pltpu.SemaphoreType.DMA((3, 2)),
                    pltpu.VMEM((rows, lk), F32), pltpu.VMEM((ds, lk), F32), pltpu.VMEM((ds, lk), F32)]
    else:
        scratch += [pltpu.SemaphoreType.DMA((2, 2)),
                    pltpu.VMEM((rows, lk), F32), pltpu.VMEM((rows, lk), F32),
                    pltpu.VMEM((KVW, past // MOBA_BLOCK), F32)]
    in_specs += [anyspec, anyspec]
    args += [k_pool, v_pool]
    o = pl.pallas_call(
        functools.partial(_sample_kernel, mode=mode, past=past, page=page, ds=ds, ck=ck, topk=topk, n_sel=n_sel),
        grid_spec=pltpu.PrefetchScalarGridSpec(
            num_scalar_prefetch=1, grid=(db,), in_specs=in_specs,
            out_specs=pl.BlockSpec((None, rows, KVW), lambda b, pt: (b, 0, 0)),
            scratch_shapes=scratch),
        out_shape=jax.ShapeDtypeStruct((db, rows, KVW), F32),
        compiler_params=pltpu.CompilerParams(dimension_semantics=("arbitrary",), vmem_limit_bytes=VMEM_LIMIT),
        name="sample_" + mode,
    )(page_table, *args)
    o5 = o.reshape(db, N_HEADS, ds, N_KV, HEAD_DIM)
    og = o5[:, np.arange(N_HEADS), :, np.arange(N_HEADS) // GRP, :]
    return jnp.transpose(og, (1, 2, 0, 3)).reshape(db, ds, WIDTH)


def _out_kernel(x_ref, gate_ref, oa_ref, za_ref, ob_ref, zb_ref, wo_ref, fg_ref, y_ref, *, final_norm):
    za = za_ref[...]
    zb = zb_ref[...]
    ma = (oa_ref[...] * (za * jax.nn.sigmoid(za))).astype(BF16)
    mb = (ob_ref[...] * (zb * jax.nn.sigmoid(zb))).astype(BF16)
    r = (jnp.dot(ma, wo_ref[0:WIDTH, :], preferred_element_type=F32)
         + jnp.dot(mb, wo_ref[WIDTH:2 * WIDTH, :], preferred_element_type=F32))
    h = x_ref[...] + gate_ref[...] * r
    if final_norm:
        h = h * lax.rsqrt(jnp.mean(h * h, axis=-1, keepdims=True) + EPS) * fg_ref[...]
    y_ref[...] = h


def _out_call(x, gate, oa, za, ob, zb, w_o, final_g, *, per_row_mod, final_norm):
    bx, s, d = x.shape
    tm = min(512, s)
    assert s % tm == 0
    row = lambda c: pl.BlockSpec((None, tm, c), lambda b, i: (b, i, 0))
    gate_spec = row(d) if per_row_mod else pl.BlockSpec((None, 1, d), lambda b, i: (b, 0, 0))
    return pl.pallas_call(
        functools.partial(_out_kernel, final_norm=final_norm),
        grid=(bx, s // tm),
        in_specs=[row(d), gate_spec, row(WIDTH), row(WIDTH), row(WIDTH), row(WIDTH),
                  pl.BlockSpec(w_o.shape, lambda b, i: (0, 0)), pl.BlockSpec((1, d), lambda b, i: (0, 0))],
        out_specs=row(d),
        out_shape=jax.ShapeDtypeStruct((bx, s, d), F32),
        compiler_params=pltpu.CompilerParams(dimension_semantics=("arbitrary", "arbitrary"),
                                             vmem_limit_bytes=VMEM_LIMIT),
        name="gate_out_residual",
    )(x, gate, oa, za, ob, zb, w_o, final_g.reshape(1, d))


def kernel(x_prompt, x_sample, c_prompt, c_sample, cache_k_a, cache_v_a, cache_idx_k, cache_k_b, cache_v_b,
           page_table, rel_bias, norm_g, w_ada, b_ada, w_in, w_o, final_g):
    bsz, seq, d = x_prompt.shape
    dbsz, dseq, _ = x_sample.shape
    depth = w_in.shape[0]
    n_pool, page = cache_k_a.shape[1], cache_k_a.shape[2]
    past = page_table.shape[1] * page
    bias_a = rel_bias[:, :N_HEADS]
    bias_b = rel_bias[:, N_HEADS:]
    topk_p = min(TOPK_MAX, seq // 4)
    topk_s = min(TOPK_MAX, (past + dseq) // 4)
    nsel_s = min(MOBA_TOPK, past // MOBA_BLOCK)

    hp = x_prompt
    hs = x_sample.reshape(1, dbsz * dseq, d)
    outs = [[] for _ in range(10)]
    for l in range(depth):
        w_l = w_in[l]
        w_cols_p = _prompt_cols(w_l).astype(BF16)
        w_rows_t = jnp.transpose(_pick_cols(w_l, _ROWS_PROMPT_T)).astype(BF16)
        w_cols_s = _pick_cols(w_l, _COLS_SAMPLE).astype(BF16)
        w_o_l = w_o[l].astype(BF16)
        last = l == depth - 1

        mod = _mod_call(jnp.concatenate([c_prompt, c_sample], axis=0), w_ada[l], b_ada[l])
        shift, scale, gate = jnp.split(mod, 3, axis=-1)

        pm = lambda t: t[:bsz].reshape(bsz, 1, d)
        (za, zb, ka_bf, kext, ki_bf, means, qat, qit, qbt, wit, kat, vat, kbt, vbt, kit, vat_bf, vbt_bf) = _proj_call(
            hp, pm(scale), pm(shift), norm_g[l], w_cols_p, w_rows_t, per_row_mod=False)
        oa = _dsa_prompt(qit, wit, qat, ki_bf, ka_bf, vat_bf, bias_a, topk_p)
        ob = _moba_prompt(qbt, kext, vbt_bf, means.reshape(bsz, seq // MOBA_BLOCK, N_KV * LANES), bias_b)
        hp = _out_call(hp, pm(gate), oa, za, ob, zb, w_o_l, final_g, per_row_mod=False, final_norm=last)

        sm = lambda t: jnp.broadcast_to(t[bsz:, None, :], (dbsz, dseq, d)).reshape(1, dbsz * dseq, d)
        (qa_s, qi_s, qb_s, za_s, zb_s, wi_s, ka_s, va_s, kb_s, vb_s, ki_s) = _proj_call(
            hs, sm(scale), sm(shift), norm_g[l], w_cols_s, None, per_row_mod=True)
        sq = lambda t: t.reshape(dbsz, dseq, t.shape[-1])
        pool = lambda c: jnp.moveaxis(c[l], 1, -1).reshape(n_pool, -1, page)
        oa_s = _sample_attn("dsa", page_table, sq(qa_s), sq(ka_s), sq(va_s), pool(cache_k_a), pool(cache_v_a), bias_a,
                            topk=topk_s, qi_s=sq(qi_s), wi_s=sq(wi_s), ki_fresh=sq(ki_s), i_pool=pool(cache_idx_k))
        ob_s = _sample_attn("moba", page_table, sq(qb_s), sq(kb_s), sq(vb_s), pool(cache_k_b), pool(cache_v_b), bias_b,
                            n_sel=nsel_s)
        flat = lambda t: t.reshape(1, dbsz * dseq, WIDTH)
        hs = _out_call(hs, sm(gate), flat(oa_s), za_s, flat(ob_s), zb_s, w_o_l, final_g, per_row_mod=True, final_norm=last)

        kv_t = lambda t: jnp.transpose(t.reshape(bsz, N_KV, HEAD_DIM, seq), (0, 3, 1, 2))
        kv = lambda t: t.reshape(dbsz, dseq, N_KV, HEAD_DIM)
        for lst, val in zip(outs, (kv_t(kat), kv_t(vat), jnp.transpose(kit, (0, 2, 1)), kv_t(kbt), kv_t(vbt),
                                   kv(ka_s), kv(va_s), ki_s.reshape(dbsz, dseq, IDX_DIM), kv(kb_s), kv(vb_s))):
            lst.append(val)

    return (hp, hs.reshape(dbsz, dseq, d)) + tuple(jnp.stack(o) for o in outs)
```

```python
import functools
import math

import numpy as np
import jax
import jax.numpy as jnp
from jax import lax
from jax.experimental import pallas as pl
from jax.experimental.pallas import tpu as pltpu

F32 = jnp.float32
BF16 = jnp.bfloat16
I32 = jnp.int32

HEAD_DIM = 64
N_HEADS = 8
N_KV = 4
GRP = N_HEADS // N_KV
WIDTH = N_HEADS * HEAD_DIM
KVW = N_KV * HEAD_DIM
IDX_HEADS = 8
IDX_DIM = 64
TOPK_MAX = 256
MOBA_BLOCK = 256
MOBA_TOPK = 3
REL_BUCKETS = 32
REL_MAX_DIST = 128
EPS = 1e-6
NEG = -1e30
LANES = 128
VMEM_LIMIT = 56 * 1024 * 1024

_ORIG = (("qa", WIDTH), ("ka", KVW), ("va", KVW), ("za", WIDTH), ("qi", IDX_HEADS * IDX_DIM), ("ki", IDX_DIM),
         ("wi", IDX_HEADS), ("qb", WIDTH), ("kb", KVW), ("vb", KVW), ("zb", WIDTH))
_COLS_SAMPLE = ("qa", "ka", "va", "za", "qi", "qb", "kb", "vb", "zb", "ki", "wi")
_ROWS_PROMPT_T = ("qa", "qi", "qb", "ka", "va", "kb", "vb", "ki", "wi")
_P_ZA, _P_ZB, _P_KA, _P_KEXT, _P_KI, _P_END = 0, WIDTH, 2 * WIDTH, 2 * WIDTH + KVW, 2 * WIDTH + KVW + N_KV * LANES, \
    2 * WIDTH + KVW + N_KV * LANES + LANES


def _col_ranges():
    off, o = {}, 0
    for name, sz in _ORIG:
        off[name] = (o, o + sz)
        o += sz
    return off


def _offsets(order):
    sizes = dict(_ORIG)
    off, o = {}, 0
    for name in order:
        off[name] = (o, o + sizes[name])
        o += sizes[name]
    return off, o


def _pick_cols(w, order):
    cols = _col_ranges()
    parts = [w[:, cols[n][0]:cols[n][1]] for n in order]
    pad = (-sum(p.shape[1] for p in parts)) % LANES
    if pad:
        parts.append(jnp.zeros((w.shape[0], pad), w.dtype))
    return jnp.concatenate(parts, axis=1)


def _prompt_cols(w):
    cols = _col_ranges()
    take = lambda n: w[:, cols[n][0]:cols[n][1]]
    z = jnp.zeros((w.shape[0], HEAD_DIM), w.dtype)
    kb = take("kb")
    parts = [take("za"), take("zb"), take("ka")]
    for g in range(N_KV):
        parts += [kb[:, g * HEAD_DIM:(g + 1) * HEAD_DIM], z]
    parts += [take("ki"), z]
    return jnp.concatenate(parts, axis=1)


def _bucket_np(n):
    n = np.maximum(np.asarray(n, np.int64), 0)
    exact = REL_BUCKETS // 2

    def large(dt):
        nf = np.maximum(n, 1).astype(dt)
        return exact + (np.log(nf / dt(exact)) / dt(math.log(REL_MAX_DIST / exact)) * dt(REL_BUCKETS - exact)).astype(np.int64)

    l32, l64 = large(np.float32), large(np.float64)
    assert np.array_equal(np.minimum(l32, REL_BUCKETS - 1)[n >= exact], np.minimum(l64, REL_BUCKETS - 1)[n >= exact])
    return np.where(n < exact, n, np.minimum(l32, REL_BUCKETS - 1)).astype(np.int32)


def _bucket_index(dist, valid):
    return jnp.asarray(np.where(valid, _bucket_np(dist), -1).astype(np.int32))


def _bias_table(bias, dist, valid):
    t = jnp.transpose(bias.astype(F32)[_bucket_np(dist)], (2, 0, 1))
    return jnp.where(jnp.asarray(valid)[None], t, -jnp.inf)


def _fill_bias_tables(bias_ref, idx_ref, tab_scr, b31_scr, tq):
    lead = idx_ref.shape[:-2]
    for g in range(N_KV):
        for j in range(GRP):
            head = g * GRP + j
            for ix in np.ndindex(*lead):
                idx = idx_ref[ix]

                def pick(k, t, idx=idx, head=head):
                    return jnp.where(idx == k, bias_ref[k, head], t)

                t = lax.fori_loop(0, REL_BUCKETS, pick, jnp.full(idx.shape, -jnp.inf, F32))
                tab_scr[(g,) + ix + (slice(None), slice(j * tq, (j + 1) * tq))] = t
            b31_scr[g, :, j * tq:(j + 1) * tq] = jnp.full((1, tq), bias_ref[REL_BUCKETS - 1, head], F32)


def _mod_kernel(c_ref, w_ref, b_ref, o_ref):
    c = c_ref[...]
    s = c * jax.nn.sigmoid(c)
    o_ref[...] = jnp.dot(s, w_ref[...], preferred_element_type=F32, precision=lax.Precision.HIGHEST) + b_ref[...]


def _mod_call(c, w_ada, b_ada):
    n, d = c.shape
    d3 = w_ada.shape[1]
    tn = 512 if d3 % 512 == 0 else d3
    return pl.pallas_call(
        _mod_kernel,
        grid=(d3 // tn,),
        in_specs=[pl.BlockSpec((n, d), lambda j: (0, 0)),
                  pl.BlockSpec((d, tn), lambda j: (0, j)),
                  pl.BlockSpec((1, tn), lambda j: (0, j))],
        out_specs=pl.BlockSpec((n, tn), lambda j: (0, j)),
        out_shape=jax.ShapeDtypeStruct((n, d3), F32),
        compiler_params=pltpu.CompilerParams(dimension_semantics=("arbitrary",), vmem_limit_bytes=VMEM_LIMIT),
        name="adaln_mod",
    )(c, w_ada, b_ada.reshape(1, d3))


def _proj_kernel(*refs, prompt, tm):
    if prompt:
        (x_ref, sc_ref, sh_ref, g_ref, w_ref, wt_ref,
         za_o, zb_o, kab_o, kext_o, kib_o, mean_o,
         qat_o, qit_o, qbt_o, wit_o, kat_o, vat_o, kbt_o, vbt_o, kit_o, vatb_o, vbtb_o) = refs
    else:
        (x_ref, sc_ref, sh_ref, g_ref, w_ref,
         qa_o, qi_o, qb_o, za_o, zb_o, wi_o, ka_o, va_o, kb_o, vb_o, ki_o) = refs
    x = x_ref[...]
    y = x * lax.rsqrt(jnp.mean(x * x, axis=-1, keepdims=True) + EPS) * g_ref[...]
    h = y * (1.0 + sc_ref[...]) + sh_ref[...]
    hb = h.astype(BF16)
    qscale = HEAD_DIM ** -0.5
    iscale = IDX_DIM ** -0.5
    wscale = IDX_HEADS ** -0.5

    def cols(a, b):
        return jnp.dot(hb, w_ref[:, a:b], preferred_element_type=F32)

    if prompt:
        za_o[...] = cols(_P_ZA, _P_ZB)
        zb_o[...] = cols(_P_ZB, _P_KA)
        kab_o[...] = cols(_P_KA, _P_KEXT).astype(BF16)
        kib_o[...] = cols(_P_KI, _P_END)[:, :IDX_DIM].astype(BF16)
        pos = pl.program_id(1) * tm + lax.broadcasted_iota(I32, (tm, LANES), 0)
        lane = lax.broadcasted_iota(I32, (tm, LANES), 1)
        onehot = jnp.where(lane - HEAD_DIM == jnp.right_shift(pos, int(math.log2(MOBA_BLOCK))), 1.0, 0.0)
        for g in range(N_KV):
            kx = cols(_P_KEXT + g * LANES, _P_KEXT + (g + 1) * LANES)
            kext_o[g] = (kx + onehot).astype(BF16)
            for r in range(tm // MOBA_BLOCK):
                mean_o[r:r + 1, g * LANES:(g + 1) * LANES] = jnp.mean(
                    kx[r * MOBA_BLOCK:(r + 1) * MOBA_BLOCK], axis=0, keepdims=True)
        roff, _ = _offsets(_ROWS_PROMPT_T)
        nt = (((1,), (1,)), ((), ()))

        def rows(name):
            a, b = roff[name]
            return lax.dot_general(wt_ref[a:b, :], hb, nt, preferred_element_type=F32)

        qat_o[...] = (rows("qa") * qscale).astype(BF16)
        qit_o[...] = (rows("qi") * iscale).astype(BF16)
        qbt_o[...] = (rows("qb") * qscale).astype(BF16)
        a = roff["wi"][0]
        wit_o[...] = lax.dot_general(wt_ref[a:a + 16, :], hb, nt, preferred_element_type=F32)[:IDX_HEADS] * wscale
        kat_o[...] = rows("ka")
        kbt_o[...] = rows("kb")
        kit_o[...] = rows("ki")
        vat = rows("va")
        vbt = rows("vb")
        vat_o[...] = vat
        vbt_o[...] = vbt
        vatb_o[...] = vat.astype(BF16)
        vbtb_o[...] = vbt.astype(BF16)
    else:
        off, _ = _offsets(_COLS_SAMPLE)
        seg = lambda name: cols(*off[name])
        qa_o[...] = (seg("qa") * qscale).astype(BF16)
        qi_o[...] = (seg("qi") * iscale).astype(BF16)
        qb_o[...] = (seg("qb") * qscale).astype(BF16)
        za_o[...] = seg("za")
        zb_o[...] = seg("zb")
        ka_o[...] = seg("ka")
        va_o[...] = seg("va")
        kb_o[...] = seg("kb")
        vb_o[...] = seg("vb")
        a, _ = off["ki"]
        kiwi = cols(a, a + LANES)
        ki_o[...] = kiwi[:, :IDX_DIM]
        wi_o[...] = kiwi[:, IDX_DIM:IDX_DIM + IDX_HEADS] * wscale


def _proj_call(x, scale, shift, g, w_cols, w_rows_t, *, per_row_mod):
    bx, s, d = x.shape
    tm = min(512, s)
    assert s % tm == 0 and tm % 8 == 0
    prompt = w_rows_t is not None
    nw = w_cols.shape[1]
    row = lambda c: pl.BlockSpec((None, tm, c), lambda b, i: (b, i, 0))
    mod_spec = row(d) if per_row_mod else pl.BlockSpec((None, 1, d), lambda b, i: (b, 0, 0))
    in_specs = [row(d), mod_spec, mod_spec,
                pl.BlockSpec((1, d), lambda b, i: (0, 0)),
                pl.BlockSpec((d, nw), lambda b, i: (0, 0))]
    args = [x, scale, shift, g.reshape(1, d), w_cols]
    sds = lambda c, dt: jax.ShapeDtypeStruct((bx, s, c), dt)
    if prompt:
        assert tm % MOBA_BLOCK == 0 and MOBA_BLOCK == 256 and nw == _P_END
        in_specs.append(pl.BlockSpec(w_rows_t.shape, lambda b, i: (0, 0)))
        args.append(w_rows_t)
        col_t = lambda r: pl.BlockSpec((None, r, tm), lambda b, i: (b, 0, i))
        sds_t = lambda r, dt: jax.ShapeDtypeStruct((bx, r, s), dt)
        nblk = tm // MOBA_BLOCK
        out_shape = [sds(WIDTH, F32), sds(WIDTH, F32), sds(KVW, BF16),
                     jax.ShapeDtypeStruct((bx, N_KV, s, LANES), BF16), sds(IDX_DIM, BF16),
                     jax.ShapeDtypeStruct((bx, s // tm, nblk, N_KV * LANES), F32),
                     sds_t(WIDTH, BF16), sds_t(WIDTH, BF16), sds_t(WIDTH, BF16), sds_t(IDX_HEADS, F32),
                     sds_t(KVW, F32), sds_t(KVW, F32), sds_t(KVW, F32), sds_t(KVW, F32), sds_t(IDX_DIM, F32),
                     sds_t(KVW, BF16), sds_t(KVW, BF16)]
        out_specs = [row(WIDTH), row(WIDTH), row(KVW),
                     pl.BlockSpec((None, N_KV, tm, LANES), lambda b, i: (b, 0, i, 0)), row(IDX_DIM),
                     pl.BlockSpec((None, None, nblk, N_KV * LANES), lambda b, i: (b, i, 0, 0)),
                     col_t(WIDTH), col_t(WIDTH), col_t(WIDTH), col_t(IDX_HEADS),
                     col_t(KVW), col_t(KVW), col_t(KVW), col_t(KVW), col_t(IDX_DIM),
                     col_t(KVW), col_t(KVW)]
    else:
        out_shape = [sds(WIDTH, BF16), sds(WIDTH, BF16), sds(WIDTH, BF16), sds(WIDTH, F32), sds(WIDTH, F32),
                     sds(IDX_HEADS, F32), sds(KVW, F32), sds(KVW, F32), sds(KVW, F32), sds(KVW, F32), sds(IDX_DIM, F32)]
        out_specs = [row(WIDTH)] * 5 + [row(IDX_HEADS)] + [row(KVW)] * 4 + [row(IDX_DIM)]
    return pl.pallas_call(
        functools.partial(_proj_kernel, prompt=prompt, tm=tm),
        grid=(bx, s // tm),
        in_specs=in_specs, out_specs=out_specs, out_shape=out_shape,
        compiler_params=pltpu.CompilerParams(dimension_semantics=("arbitrary", "arbitrary"),
                                             vmem_limit_bytes=VMEM_LIMIT),
        name="norm_mod_proj",
    )(*args)


ONES_ROWS = 16


def _flash_init(m_scr, acc_scr):
    m_scr[...] = jnp.full(m_scr.shape, NEG, F32)
    acc_scr[...] = jnp.zeros(acc_scr.shape, F32)


def _flash_update(sts, shifts, vts, m_scr, acc_scr):
    ps, alphas = [], []
    for g, (st, shift) in enumerate(zip(sts, shifts)):
        m_old = m_scr[g]
        mb = jnp.max(st, axis=0, keepdims=True)
        if shift is not None:
            mb = mb + shift
        m_new = jnp.maximum(m_old, mb)
        m_scr[g] = m_new
        alphas.append(jnp.exp(m_old - m_new))
        ps.append(jnp.exp((st - (m_new if shift is None else m_new - shift)).astype(BF16)))
    for g, (p, alpha, vt) in enumerate(zip(ps, alphas, vts)):
        vt1 = jnp.concatenate([vt, jnp.ones((ONES_ROWS, vt.shape[1]), BF16)], axis=0)
        acc_scr[g] = alpha * acc_scr[g] + jnp.dot(vt1, p, preferred_element_type=F32)


def _flash_finish(o_ref, acc_scr, tq):
    for g in range(N_KV):
        acc = acc_scr[g]
        o = jnp.transpose(acc[:LANES] / acc[LANES:LANES + 1])
        c0 = (g % 2) * HEAD_DIM
        for j in range(GRP):
            hd = g * GRP + j
            o_ref[:, hd * HEAD_DIM:(hd + 1) * HEAD_DIM] = o[j * tq:(j + 1) * tq, c0:c0 + HEAD_DIM]


def _f2key(x):
    b = lax.bitcast_convert_type(x, I32)
    return b ^ ((b >> 31) & jnp.int32(0x7FFFFFFF))


def _key2f(k):
    return lax.bitcast_convert_type(k ^ ((k >> 31) & jnp.int32(0x7FFFFFFF)), F32)


def _kth_threshold(count_gt, vmax, small, k, nvis):
    kf = jnp.float32(k)
    logk = jnp.float32(math.log(k))
    lo0 = jnp.full(vmax.shape, _f2key(jnp.float32(-jnp.inf)), I32)
    hi0 = jnp.where(small, lo0, _f2key(vmax))
    st0 = (jnp.int32(0), lo0, hi0, nvis.astype(F32), jnp.zeros(vmax.shape, F32), small.astype(I32))

    def cond(st):
        return jnp.logical_and(st[0] < 160, jnp.min(st[5]) == 0)

    def body(st):
        it, lo, hi, clo, chi, done = st
        mid_key = (lo & hi) + ((lo ^ hi) >> 1)
        conv = mid_key == lo
        lof = _key2f(lo)
        hif = _key2f(hi)
        llo = jnp.log(clo)
        w = (llo - logk) / (llo - jnp.log(jnp.maximum(chi, 0.5)))
        phase = lax.rem(it, 4)
        probe = jnp.where(phase == 1, 0.5 * lof + 0.5 * hif, lof + w * (hif - lof))
        pk = _f2key(probe)
        inside = jnp.logical_and(jnp.logical_and(pk > lo, pk < hi), lof > -jnp.inf)
        off_zero = jnp.logical_or(lof >= 0.0, hif < 0.0)
        use = jnp.logical_and(jnp.logical_and(inside, off_zero), phase != 3)
        mid = jnp.where(use, pk, mid_key)
        first_at_zero = jnp.logical_and(it == 0, jnp.logical_and(lo < 0, hi > 0))
        mid = jnp.where(first_at_zero, 0, mid)
        cnt = count_gt(_key2f(mid))
        active = jnp.logical_and(done == 0, jnp.logical_not(conv))
        up = jnp.logical_and(active, cnt > kf)
        down = jnp.logical_and(active, cnt <= kf)
        lo = jnp.where(up, mid, lo)
        clo = jnp.where(up, cnt, clo)
        hi = jnp.where(down, mid, hi)
        chi = jnp.where(down, cnt, chi)
        fin = jnp.logical_or(conv, jnp.logical_and(active, cnt == kf))
        return it + 1, lo, hi, clo, chi, jnp.where(fin, 1, done)

    _, _, hi, _, chi, _ = lax.while_loop(cond, body, st0)
    return _key2f(hi), chi


def _dsa_prompt_kernel(bias_ref, qit_ref, wit_ref, qat_ref, ki_ref, ka_ref, vat_ref, idx_ref, o_ref,
                       s_scr, qis_scr, qext_scr, tab_scr, b31_scr, m_scr, acc_scr, *, tq, fw, topk):
    i = pl.program_id(1)
    q0 = i * tq
    rr = GRP * tq

    @pl.when(jnp.logical_and(pl.program_id(0) == 0, i == 0))
    def _():
        _fill_bias_tables(bias_ref, idx_ref, tab_scr, b31_scr, tq)

    qit = qit_ref[...]
    qat = qat_ref[...]
    wit = wit_ref[...]
    for h in range(IDX_HEADS):
        qis_scr[:, h * tq:(h + 1) * tq] = qit[h * IDX_DIM:(h + 1) * IDX_DIM, :]
    for g in range(N_KV):
        r0 = (g % 2) * HEAD_DIM
        qext_scr[g, HEAD_DIM - r0:2 * HEAD_DIM - r0, :] = jnp.zeros((HEAD_DIM, rr), BF16)
        for j in range(GRP):
            hd = g * GRP + j
            qext_scr[g, r0:r0 + HEAD_DIM, j * tq:(j + 1) * tq] = qat[hd * HEAD_DIM:(hd + 1) * HEAD_DIM, :]

    qpos = q0 + lax.broadcasted_iota(I32, (1, tq), 1)

    nb1 = (q0 + tq + fw - 1) // fw

    def p1(jb, vmax):
        k0 = pl.multiple_of(jb * fw, fw)
        d = jnp.dot(ki_ref[pl.ds(k0, fw), :], qis_scr[...], preferred_element_type=F32)
        acc = jnp.maximum(d[:, 0:tq], 0.0) * wit[0:1, :]
        for h in range(1, IDX_HEADS):
            acc = acc + jnp.maximum(d[:, h * tq:(h + 1) * tq], 0.0) * wit[h:h + 1, :]
        key = k0 + lax.broadcasted_iota(I32, (fw, tq), 0)
        sc = jnp.where(key <= qpos, acc, -jnp.inf)
        s_scr[pl.ds(k0, fw), :] = sc
        return jnp.maximum(vmax, jnp.max(sc, axis=0, keepdims=True))

    vmax = lax.fori_loop(0, nb1, p1, jnp.full((1, tq), -jnp.inf, F32))

    def count_gt(t):
        def cb(jb, c):
            x = s_scr[pl.ds(pl.multiple_of(jb * fw, fw), fw), :]
            one = jnp.where(x > t, 1.0, 0.0)
            return c + jnp.sum(one.reshape(fw // 64, 64, tq), axis=0)
        c = lax.fori_loop(0, nb1, cb, jnp.zeros((64, tq), F32))
        return jnp.sum(c, axis=0, keepdims=True)

    small = qpos < topk
    hi, chi = _kth_threshold(count_gt, vmax, small, topk, qpos + 1)
    need = jnp.where(small, 0.0, jnp.float32(topk) - chi)
    any_tie = jnp.max(need) > 0.0

    _flash_init(m_scr, acc_scr)
    nfar = jnp.maximum(q0 - LANES, 0) // fw
    near0 = nfar * (fw // LANES)

    def select(x, eqb, with_ties, width):
        gt = x > hi
        if not with_ties:
            return gt, eqb
        eq = x == hi
        e = jnp.where(eq, 1.0, 0.0)
        r = lax.broadcasted_iota(I32, (width, width), 0)
        c = lax.broadcasted_iota(I32, (width, width), 1)
        lower = jnp.where(c < r, 1.0, 0.0).astype(BF16)
        rank = jnp.dot(lower, e.astype(BF16), preferred_element_type=F32) + eqb
        sel = jnp.logical_or(gt, jnp.logical_and(eq, rank < need))
        return sel, eqb + jnp.sum(e, axis=0, keepdims=True)

    def attend(with_ties):
        def masked_qk(k0, width, eqb):
            sel, eqb = select(s_scr[pl.ds(k0, width), :], eqb, with_ties, width)
            madd = jnp.where(sel, 0.0, -jnp.inf)
            madd = jnp.concatenate([madd] * GRP, axis=1)
            sts = tuple(jnp.dot(ka_ref[pl.ds(k0, width), (g // 2) * LANES:(g // 2 + 1) * LANES], qext_scr[g],
                                preferred_element_type=F32) + madd for g in range(N_KV))
            return sts, eqb

        def softmax_pv(k0, width, sts, t):
            vts = [vat_ref[(g // 2) * LANES:(g // 2 + 1) * LANES, pl.ds(k0, width)] for g in range(N_KV)]
            if t is None:
                _flash_update(sts, [b31_scr[g] for g in range(N_KV)], vts, m_scr, acc_scr)
            else:
                _flash_update([st + tab_scr[g, t] for g, st in enumerate(sts)], [None] * N_KV, vts,
                              m_scr, acc_scr)

        def far(jb, eqb):
            k0 = pl.multiple_of(jb * fw, fw)
            sts, eqb = masked_qk(k0, fw, eqb)
            softmax_pv(k0, fw, sts, None)
            return eqb

        def near(jj, eqb):
            k0 = pl.multiple_of(jj * LANES, LANES)
            sts, eqb = masked_qk(k0, LANES, eqb)
            softmax_pv(k0, LANES, sts, jnp.clip(jj - i + 2, 0, 2))
            return eqb

        def far2(m, eqb):
            return far(2 * m + 1, far(2 * m, eqb))

        eqb = lax.fori_loop(0, nfar // 2, far2, jnp.zeros((1, tq), F32))
        eqb = lax.cond(nfar % 2 == 1, lambda e: far(nfar - 1, e), lambda e: e, eqb)
        lax.fori_loop(near0, i + 1, near, eqb)

    @pl.when(any_tie)
    def _():
        attend(True)

    @pl.when(jnp.logical_not(any_tie))
    def _():
        attend(False)

    _flash_finish(o_ref, acc_scr, tq)


def _dsa_prompt(qit, wit, qat, ki_bf, ka_bf, vat_bf, bias_a, topk):
    bx, _, s = qat.shape
    tq = LANES
    fw = min(512, s)
    rr = GRP * tq
    assert s % fw == 0 and s % tq == 0 and REL_MAX_DIST <= LANES
    c = np.arange(LANES)[:, None]
    r = np.arange(tq)[None, :]
    ones = np.ones((LANES, tq), bool)
    idx = jnp.stack([_bucket_index(np.full((LANES, tq), REL_MAX_DIST), ones),
                     _bucket_index(r - c + LANES, ones),
                     _bucket_index(r - c, (r - c) >= 0)])
    res = lambda shape: pl.BlockSpec((None,) + shape, lambda b, i: (b, 0, 0), pipeline_mode=pl.Buffered(1))
    blk_t = lambda rows: pl.BlockSpec((None, rows, tq), lambda b, i: (b, 0, i))
    return pl.pallas_call(
        functools.partial(_dsa_prompt_kernel, tq=tq, fw=fw, topk=topk),
        grid=(bx, s // tq),
        in_specs=[pl.BlockSpec(memory_space=pltpu.SMEM),
                  blk_t(IDX_HEADS * IDX_DIM), blk_t(IDX_HEADS), blk_t(WIDTH),
                  res((s, IDX_DIM)), res((s, KVW)), res((KVW, s)),
                  pl.BlockSpec(idx.shape, lambda b, i: (0, 0, 0), pipeline_mode=pl.Buffered(1))],
        out_specs=pl.BlockSpec((None, tq, WIDTH), lambda b, i: (b, i, 0)),
        out_shape=jax.ShapeDtypeStruct((bx, s, WIDTH), F32),
        scratch_shapes=[pltpu.VMEM((s, tq), F32),
                        pltpu.VMEM((IDX_DIM, IDX_HEADS * tq), BF16),
                        pltpu.VMEM((N_KV, LANES, rr), BF16),
                        pltpu.VMEM((N_KV, 3, LANES, rr), F32),
                        pltpu.VMEM((N_KV, 1, rr), F32),
                        pltpu.VMEM((N_KV, 1, rr), F32),
                        pltpu.VMEM((N_KV, LANES + ONES_ROWS, rr), F32)],
        compiler_params=pltpu.CompilerParams(dimension_semantics=("arbitrary", "arbitrary"),
                                             vmem_limit_bytes=VMEM_LIMIT),
        name="dsa_prompt",
    )(bias_a.astype(F32), qit, wit, qat, ki_bf, ka_bf, vat_bf, idx)


def _top_blocks(gate, blk, n_valid, n_sel, axis):
    nb = gate.shape[axis]
    gate = jnp.where(blk < n_valid, gate, -jnp.inf)
    sel = jnp.zeros(gate.shape, jnp.bool_)
    for _ in range(n_sel):
        mx = jnp.max(gate, axis=axis, keepdims=True)
        first = jnp.min(jnp.where(gate == mx, blk, nb), axis=axis, keepdims=True)
        pick = blk == first
        sel = jnp.logical_or(sel, pick)
        gate = jnp.where(pick, -jnp.inf, gate)
    return jnp.logical_and(sel, blk < n_valid)


def _moba_prompt_kernel(bias_ref, qbt_ref, kext_ref, vbt_ref, mean_ref, idx_ref, o_ref,
                        qext_scr, tab_scr, b31_scr, m_scr, acc_scr, *, tq, nbp, n_sel):
    i = pl.program_id(1)
    q0 = i * tq
    own = q0 // MOBA_BLOCK
    par = (q0 % MOBA_BLOCK) // tq
    rr = GRP * tq

    @pl.when(jnp.logical_and(pl.program_id(0) == 0, i == 0))
    def _():
        _fill_bias_tables(bias_ref, idx_ref, tab_scr, b31_scr, tq)

    qbt = qbt_ref[...]
    blk = lax.broadcasted_iota(I32, (nbp, rr), 0)
    zeros = jnp.zeros((HEAD_DIM, rr), BF16)
    for g in range(N_KV):
        qg = jnp.concatenate([qbt[(g * GRP + j) * HEAD_DIM:(g * GRP + j + 1) * HEAD_DIM, :] for j in range(GRP)], axis=1)
        mg = mean_ref[:, g * LANES:(g + 1) * LANES].astype(BF16)
        gate = jnp.dot(mg, jnp.concatenate([qg, zeros], axis=0), preferred_element_type=F32)
        ok = jnp.logical_or(_top_blocks(gate, blk, own, n_sel, 0), blk == own)
        qext_scr[g, 0:HEAD_DIM, :] = qg
        qext_scr[g, HEAD_DIM:HEAD_DIM + nbp, :] = jnp.where(ok, 0.0, NEG).astype(BF16)

    _flash_init(m_scr, acc_scr)

    mb = MOBA_BLOCK

    def attend(c0, width, tabs):
        sts = [jnp.dot(kext_ref[g, pl.ds(c0, width), :], qext_scr[g], preferred_element_type=F32)
               for g in range(N_KV)]
        vts = [vbt_ref[(g // 2) * LANES:(g // 2 + 1) * LANES, pl.ds(c0, width)] for g in range(N_KV)]
        if tabs is None:
            _flash_update(sts, [b31_scr[g] for g in range(N_KV)], vts, m_scr, acc_scr)
        else:
            _flash_update([st + t for st, t in zip(sts, tabs)], [None] * N_KV, vts, m_scr, acc_scr)

    nfar = jnp.maximum(own - 1, 0)

    def far4(m, carry):
        c0 = pl.multiple_of(m * 4 * mb, 4 * mb)
        attend(c0, 2 * mb, None)
        attend(c0 + 2 * mb, 2 * mb, None)
        return carry

    lax.fori_loop(0, nfar // 4, far4, 0)

    @pl.when((nfar // 2) % 2 == 1)
    def _():
        attend(pl.multiple_of((nfar // 4) * 4 * mb, 2 * mb), 2 * mb, None)

    @pl.when(nfar % 2 == 1)
    def _():
        attend(pl.multiple_of((nfar - 1) * mb, mb), mb, None)

    @pl.when(own >= 1)
    def _():
        attend(pl.multiple_of((own - 1) * mb, mb), 2 * mb, [tab_scr[g, par] for g in range(N_KV)])

    @pl.when(own == 0)
    def _():
        attend(0, mb, [tab_scr[g, par, mb:2 * mb] for g in range(N_KV)])

    _flash_finish(o_ref, acc_scr, tq)


def _moba_prompt(qbt, kext, vbt_bf, means, bias_b):
    bx, _, s = qbt.shape
    tq = LANES
    nb = s // MOBA_BLOCK
    nbp = HEAD_DIM
    rr = GRP * tq
    npar = MOBA_BLOCK // tq
    assert s % MOBA_BLOCK == 0 and nb <= nbp and MOBA_BLOCK % tq == 0 and REL_MAX_DIST <= MOBA_BLOCK
    n_sel = min(MOBA_TOPK, nb - 1)
    means = jnp.pad(means, ((0, 0), (0, nbp - nb), (0, 0)))
    c = np.arange(MOBA_BLOCK)[:, None]
    r = np.arange(tq)[None, :]
    slabs = []
    for par in range(npar):
        d_own = par * tq + r - c
        slabs.append(jnp.concatenate([_bucket_index(d_own + MOBA_BLOCK, np.ones_like(d_own, bool)),
                                      _bucket_index(d_own, d_own >= 0)], axis=0))
    idx = jnp.stack(slabs)
    res = lambda shape: pl.BlockSpec((None,) + shape, lambda b, i: (b,) + (0,) * len(shape),
                                     pipeline_mode=pl.Buffered(1))
    return pl.pallas_call(
        functools.partial(_moba_prompt_kernel, tq=tq, nbp=nbp, n_sel=n_sel),
        grid=(bx, s // tq),
        in_specs=[pl.BlockSpec(memory_space=pltpu.SMEM),
                  pl.BlockSpec((None, WIDTH, tq), lambda b, i: (b, 0, i)),
                  res((N_KV, s, LANES)), res((KVW, s)), res((nbp, N_KV * LANES)),
                  pl.BlockSpec(idx.shape, lambda b, i: (0, 0, 0), pipeline_mode=pl.Buffered(1))],
        out_specs=pl.BlockSpec((None, tq, WIDTH), lambda b, i: (b, i, 0)),
        out_shape=jax.ShapeDtypeStruct((bx, s, WIDTH), F32),
        scratch_shapes=[pltpu.VMEM((N_KV, HEAD_DIM + nbp, rr), BF16),
                        pltpu.VMEM((N_KV, npar, 2 * MOBA_BLOCK, rr), F32),
                        pltpu.VMEM((N_KV, 1, rr), F32),
                        pltpu.VMEM((N_KV, 1, rr), F32),
                        pltpu.VMEM((N_KV, LANES + ONES_ROWS, rr), F32)],
        compiler_params=pltpu.CompilerParams(dimension_semantics=("arbitrary", "arbitrary"),
                                             vmem_limit_bytes=VMEM_LIMIT),
        name="moba_prompt",
    )(bias_b.astype(F32), qbt, kext, vbt_bf, means, idx)


def _sample_kernel(*refs, mode, past, page, ds, ck, topk, n_sel):
    if mode == "dsa":
        (pt_ref, qbd_ref, kf_ref, vf_ref, btab_ref, b31_ref, qis_ref, wic_ref, kif_ref,
         ipool, kpool, vpool, o_ref, kbuf, vbuf, ibuf, sem, lg_scr, madd_scr, sc_scr) = refs
        pools, bufs = (kpool, vpool, ipool), (kbuf, vbuf, ibuf)
    else:
        (pt_ref, qbd_ref, kf_ref, vf_ref, btab_ref, b31_ref,
         kpool, vpool, o_ref, kbuf, vbuf, sem, lg_scr, madd_scr, mean_scr) = refs
        pools, bufs = (kpool, vpool), (kbuf, vbuf)
    b = pl.program_id(0)
    nseq = pl.num_programs(0)
    slot = b % 2
    npages = past // page
    nck = past // ck
    lk = past + LANES
    rows = N_HEADS * ds
    nt = (((1,), (1,)), ((), ()))

    def copies(seq, sl):
        out = []
        for a, (pool, buf) in enumerate(zip(pools, bufs)):
            def mk(pg, pool=pool, buf=buf, a=a):
                return pltpu.make_async_copy(pool.at[pt_ref[seq, pg]],
                                             buf.at[sl, :, pl.ds(pl.multiple_of(pg * page, page), page)],
                                             sem.at[a, sl])
            out.append(mk)
        return out

    def start_all(seq, sl):
        def body(pg, c):
            for mk in copies(seq, sl):
                mk(pg).start()
            return c
        lax.fori_loop(0, npages, body, 0)

    def wait_all(seq, sl):
        def body(pg, c):
            for mk in copies(seq, sl):
                mk(pg).wait()
            return c
        lax.fori_loop(0, npages, body, 0)

    @pl.when(b == 0)
    def _():
        start_all(b, slot)

    @pl.when(b + 1 < nseq)
    def _():
        start_all(b + 1, 1 - slot)

    wait_all(b, slot)

    def tail(ref):
        return jnp.concatenate([ref[...], jnp.zeros((LANES - ds, ref.shape[1]), F32)], axis=0).astype(BF16)

    kf_tail = tail(kf_ref)
    vf_tail = tail(vf_ref)
    qrow = lax.broadcasted_iota(I32, (ds, 1), 0)

    if mode == "dsa":
        qis = qis_ref[...]
        wic = wic_ref[...]

        def head_sum(d):
            t = jnp.maximum(d, 0.0) * wic
            return jnp.sum(t.reshape(IDX_HEADS, ds, d.shape[1]), axis=0)

        def sc_body(c, carry):
            r0 = pl.multiple_of(c * ck, ck)
            d = jnp.dot(qis, ibuf[slot, :, pl.ds(r0, ck)].astype(BF16), preferred_element_type=F32)
            sc_scr[:, pl.ds(r0, ck)] = head_sum(d)
            return carry

        lax.fori_loop(0, nck, sc_body, 0)
        lane = lax.broadcasted_iota(I32, (ds, LANES), 1)
        d_tail = lax.dot_general(qis, tail(kif_ref), nt, preferred_element_type=F32)
        sc_scr[:, past:lk] = jnp.where(lane <= qrow, head_sum(d_tail), -jnp.inf)

        x = sc_scr[...]
        rowmax = jnp.max(x, axis=1, keepdims=True)
        small = (past + qrow) < topk

        def count_gt(t):
            return jnp.sum(jnp.where(sc_scr[...] > t, 1.0, 0.0), axis=1, keepdims=True)

        hi, chi = _kth_threshold(count_gt, rowmax, small, topk, past + qrow + 1)
        need = jnp.where(small, 0.0, jnp.float32(topk) - chi)
        any_tie = jnp.max(need) > 0.0

        @pl.when(jnp.logical_not(any_tie))
        def _():
            madd_scr[...] = jnp.where(sc_scr[...] > hi, 0.0, -jnp.inf)

        @pl.when(any_tie)
        def _():
            r = lax.broadcasted_iota(I32, (LANES, LANES), 0)
            c = lax.broadcasted_iota(I32, (LANES, LANES), 1)
            upper = jnp.where(r < c, 1.0, 0.0).astype(BF16)

            def tb(cix, eqb):
                c0 = pl.multiple_of(cix * LANES, LANES)
                xx = sc_scr[:, pl.ds(c0, LANES)]
                eq = xx == hi
                e = jnp.where(eq, 1.0, 0.0)
                rank = jnp.dot(e.astype(BF16), upper, preferred_element_type=F32) + eqb
                sel = jnp.logical_or(xx > hi, jnp.logical_and(eq, rank < need))
                madd_scr[:, pl.ds(c0, LANES)] = jnp.where(sel, 0.0, -jnp.inf)
                return eqb + jnp.sum(e, axis=1, keepdims=True)

            lax.fori_loop(0, lk // LANES, tb, jnp.zeros((ds, 1), F32))

        def madd_at(r0, width):
            m = madd_scr[:, pl.ds(r0, width)]
            return jnp.broadcast_to(m[None], (N_HEADS, ds, width)).reshape(rows, width)
    else:
        nbs = past // MOBA_BLOCK
        for n in range(nbs):
            mean_scr[:, n:n + 1] = jnp.mean(kbuf[slot, :, n * MOBA_BLOCK:(n + 1) * MOBA_BLOCK], axis=1, keepdims=True)
        gate = jnp.dot(qbd_ref[...], mean_scr[...].astype(BF16), preferred_element_type=F32)
        blk = lax.broadcasted_iota(I32, (rows, nbs), 1)
        ok = _top_blocks(gate, blk, nbs, n_sel, 1)
        mv = jnp.where(ok, 0.0, -jnp.inf)
        for n in range(nbs):
            madd_scr[:, n * MOBA_BLOCK:(n + 1) * MOBA_BLOCK] = jnp.broadcast_to(mv[:, n:n + 1], (rows, MOBA_BLOCK))
        madd_scr[:, past:lk] = jnp.zeros((rows, LANES), F32)

        def madd_at(r0, width):
            return madd_scr[:, pl.ds(r0, width)]

    qbd = qbd_ref[...]
    b31 = b31_ref[...]

    def lg_body(c, m):
        r0 = pl.multiple_of(c * ck, ck)
        lg = jnp.dot(qbd, kbuf[slot, :, pl.ds(r0, ck)].astype(BF16), preferred_element_type=F32)
        lg = lg + madd_at(r0, ck) + jnp.where(c == nck - 1, btab_ref[:, 0:ck], b31)
        lg_scr[:, pl.ds(r0, ck)] = lg
        return jnp.maximum(m, jnp.max(lg, axis=1, keepdims=True))

    m = lax.fori_loop(0, nck, lg_body, jnp.full((rows, 1), NEG, F32))
    lgt = (lax.dot_general(qbd, kf_tail, nt, preferred_element_type=F32)
           + madd_at(past, LANES) + btab_ref[:, ck:ck + LANES])
    m = jnp.maximum(m, jnp.max(lgt, axis=1, keepdims=True))

    def pv_body(c, carry):
        l, acc = carry
        r0 = pl.multiple_of(c * ck, ck)
        p = jnp.exp(lg_scr[:, pl.ds(r0, ck)] - m)
        vc = vbuf[slot, :, pl.ds(r0, ck)].astype(BF16)
        return (l + jnp.sum(p, axis=1, keepdims=True),
                acc + lax.dot_general(p.astype(BF16), vc, nt, preferred_element_type=F32))

    l, acc = lax.fori_loop(0, nck, pv_body, (jnp.zeros((rows, 1), F32), jnp.zeros((rows, KVW), F32)))
    pt = jnp.exp(lgt - m)
    l = l + jnp.sum(pt, axis=1, keepdims=True)
    acc = acc + jnp.dot(pt.astype(BF16), vf_tail, preferred_element_type=F32)
    o_ref[...] = acc / l


def _sample_attn(mode, page_table, q_s, k_fresh, v_fresh, k_pool, v_pool, bias, *, topk=0, n_sel=0,
                 qi_s=None, wi_s=None, ki_fresh=None, i_pool=None):
    db, ds, _ = q_s.shape
    npages = page_table.shape[1]
    page = k_pool.shape[2]
    past = npages * page
    ck = min(1024, past)
    rows = N_HEADS * ds
    lk = past + LANES
    assert past % ck == 0 and ck >= REL_MAX_DIST + ds and ds <= LANES and past % MOBA_BLOCK == 0 and ds % 8 == 0

    qh = jnp.transpose(q_s.reshape(db, ds, N_HEADS, HEAD_DIM), (0, 2, 1, 3))
    onehot = jnp.asarray((np.arange(N_HEADS)[:, None] // GRP == np.arange(N_KV)[None, :]).astype(np.float32), BF16)
    qbd = (qh[:, :, :, None, :] * onehot[None, :, None, :, None]).reshape(db, rows, KVW)

    hq = np.arange(ds)[:, None]
    c = np.arange(ck + LANES)[None, :]
    dist = np.where(c < ck, ck + hq - c, hq - (c - ck))
    valid = np.where(c < ck, True, (c - ck) <= hq)
    btab = _bias_table(bias, dist, valid).reshape(rows, ck + LANES)
    b31 = jnp.repeat(bias[REL_BUCKETS - 1].astype(F32), ds).reshape(rows, 1)

    per = lambda shape: pl.BlockSpec((None,) + shape, lambda b, pt: (b, 0, 0))
    const = lambda a: pl.BlockSpec(a.shape, lambda b, pt: (0,) * a.ndim)
    anyspec = pl.BlockSpec(memory_space=pl.ANY)
    in_specs = [per((rows, KVW)), per((ds, KVW)), per((ds, KVW)), const(btab), const(b31)]
    args = [qbd, k_fresh, v_fresh, btab, b31]
    scratch = [pltpu.VMEM((2, KVW, past), F32), pltpu.VMEM((2, KVW, past), F32)]
    if mode == "dsa":
        qis = jnp.transpose(qi_s.reshape(db, ds, IDX_HEADS, IDX_DIM), (0, 2, 1, 3)).reshape(db, IDX_HEADS * ds, IDX_DIM)
        wic = jnp.transpose(wi_s, (0, 2, 1)).reshape(db, IDX_HEADS * ds, 1)
        in_specs += [per((IDX_HEADS * ds, IDX_DIM)), per((IDX_HEADS * ds, 1)), per((ds, IDX_DIM)), anyspec]
        args += [qis, wic, ki_fresh, i_pool]
        scratch += [pltpu.VMEM((2, IDX_DIM, past), F32), pltpu.SemaphoreType.DMA((3, 2)),
                    pltpu.VMEM((rows, lk), F32), pltpu.VMEM((ds, lk), F32), pltpu.VMEM((ds, lk), F32)]
    else:
        scratch += [pltpu.SemaphoreType.DMA((2, 2)),
                    pltpu.VMEM((rows, lk), F32), pltpu.VMEM((rows, lk), F32),
                    pltpu.VMEM((KVW, past // MOBA_BLOCK), F32)]
    in_specs += [anyspec, anyspec]
    args += [k_pool, v_pool]
    o = pl.pallas_call(
        functools.partial(_sample_kernel, mode=mode, past=past, page=page, ds=ds, ck=ck, topk=topk, n_sel=n_sel),
        grid_spec=pltpu.PrefetchScalarGridSpec(
            num_scalar_prefetch=1, grid=(db,), in_specs=in_specs,
            out_specs=pl.BlockSpec((None, rows, KVW), lambda b, pt: (b, 0, 0)),
            scratch_shapes=scratch),
        out_shape=jax.ShapeDtypeStruct((db, rows, KVW), F32),
        compiler_params=pltpu.CompilerParams(dimension_semantics=("arbitrary",), vmem_limit_bytes=VMEM_LIMIT),
        name="sample_" + mode,
    )(page_table, *args)
    o5 = o.reshape(db, N_HEADS, ds, N_KV, HEAD_DIM)
    og = o5[:, np.arange(N_HEADS), :, np.arange(N_HEADS) // GRP, :]
    return jnp.transpose(og, (1, 2, 0, 3)).reshape(db, ds, WIDTH)


def _out_kernel(x_ref, gate_ref, oa_ref, za_ref, ob_ref, zb_ref, wo_ref, fg_ref, y_ref, *, final_norm):
    za = za_ref[...]
    zb = zb_ref[...]
    ma = (oa_ref[...] * (za * jax.nn.sigmoid(za))).astype(BF16)
    mb = (ob_ref[...] * (zb * jax.nn.sigmoid(zb))).astype(BF16)
    r = (jnp.dot(ma, wo_ref[0:WIDTH, :], preferred_element_type=F32)
         + jnp.dot(mb, wo_ref[WIDTH:2 * WIDTH, :], preferred_element_type=F32))
    h = x_ref[...] + gate_ref[...] * r
    if final_norm:
        h = h * lax.rsqrt(jnp.mean(h * h, axis=-1, keepdims=True) + EPS) * fg_ref[...]
    y_ref[...] = h


def _out_call(x, gate, oa, za, ob, zb, w_o, final_g, *, per_row_mod, final_norm):
    bx, s, d = x.shape
    tm = min(512, s)
    assert s % tm == 0
    row = lambda c: pl.BlockSpec((None, tm, c), lambda b, i: (b, i, 0))
    gate_spec = row(d) if per_row_mod else pl.BlockSpec((None, 1, d), lambda b, i: (b, 0, 0))
    return pl.pallas_call(
        functools.partial(_out_kernel, final_norm=final_norm),
        grid=(bx, s // tm),
        in_specs=[row(d), gate_spec, row(WIDTH), row(WIDTH), row(WIDTH), row(WIDTH),
                  pl.BlockSpec(w_o.shape, lambda b, i: (0, 0)), pl.BlockSpec((1, d), lambda b, i: (0, 0))],
        out_specs=row(d),
        out_shape=jax.ShapeDtypeStruct((bx, s, d), F32),
        compiler_params=pltpu.CompilerParams(dimension_semantics=("arbitrary", "arbitrary"),
                                             vmem_limit_bytes=VMEM_LIMIT),
        name="gate_out_residual",
    )(x, gate, oa, za, ob, zb, w_o, final_g.reshape(1, d))


def kernel(x_prompt, x_sample, c_prompt, c_sample, cache_k_a, cache_v_a, cache_idx_k, cache_k_b, cache_v_b,
           page_table, rel_bias, norm_g, w_ada, b_ada, w_in, w_o, final_g):
    bsz, seq, d = x_prompt.shape
    dbsz, dseq, _ = x_sample.shape
    depth = w_in.shape[0]
    n_pool, page = cache_k_a.shape[1], cache_k_a.shape[2]
    past = page_table.shape[1] * page
    bias_a = rel_bias[:, :N_HEADS]
    bias_b = rel_bias[:, N_HEADS:]
    topk_p = min(TOPK_MAX, seq // 4)
    topk_s = min(TOPK_MAX, (past + dseq) // 4)
    nsel_s = min(MOBA_TOPK, past // MOBA_BLOCK)

    hp = x_prompt
    hs = x_sample.reshape(1, dbsz * dseq, d)
    outs = [[] for _ in range(10)]
    for l in range(depth):
        w_l = w_in[l]
        w_cols_p = _prompt_cols(w_l).astype(BF16)
        w_rows_t = jnp.transpose(_pick_cols(w_l, _ROWS_PROMPT_T)).astype(BF16)
        w_cols_s = _pick_cols(w_l, _COLS_SAMPLE).astype(BF16)
        w_o_l = w_o[l].astype(BF16)
        last = l == depth - 1

        mod = _mod_call(jnp.concatenate([c_prompt, c_sample], axis=0), w_ada[l], b_ada[l])
        shift, scale, gate = jnp.split(mod, 3, axis=-1)

        pm = lambda t: t[:bsz].reshape(bsz, 1, d)
        (za, zb, ka_bf, kext, ki_bf, means, qat, qit, qbt, wit, kat, vat, kbt, vbt, kit, vat_bf, vbt_bf) = _proj_call(
            hp, pm(scale), pm(shift), norm_g[l], w_cols_p, w_rows_t, per_row_mod=False)
        oa = _dsa_prompt(qit, wit, qat, ki_bf, ka_bf, vat_bf, bias_a, topk_p)
        ob = _moba_prompt(qbt, kext, vbt_bf, means.reshape(bsz, seq // MOBA_BLOCK, N_KV * LANES), bias_b)
        hp = _out_call(hp, pm(gate), oa, za, ob, zb, w_o_l, final_g, per_row_mod=False, final_norm=last)

        sm = lambda t: jnp.broadcast_to(t[bsz:, None, :], (dbsz, dseq, d)).reshape(1, dbsz * dseq, d)
        (qa_s, qi_s, qb_s, za_s, zb_s, wi_s, ka_s, va_s, kb_s, vb_s, ki_s) = _proj_call(
            hs, sm(scale), sm(shift), norm_g[l], w_cols_s, None, per_row_mod=True)
        sq = lambda t: t.reshape(dbsz, dseq, t.shape[-1])
        pool = lambda c: jnp.moveaxis(c[l], 1, -1).reshape(n_pool, -1, page)
        oa_s = _sample_attn("dsa", page_table, sq(qa_s), sq(ka_s), sq(va_s), pool(cache_k_a), pool(cache_v_a), bias_a,
                            topk=topk_s, qi_s=sq(qi_s), wi_s=sq(wi_s), ki_fresh=sq(ki_s), i_pool=pool(cache_idx_k))
        ob_s = _sample_attn("moba", page_table, sq(qb_s), sq(kb_s), sq(vb_s), pool(cache_k_b), pool(cache_v_b), bias_b,
                            n_sel=nsel_s)
        flat = lambda t: t.reshape(1, dbsz * dseq, WIDTH)
        hs = _out_call(hs, sm(gate), flat(oa_s), za_s, flat(ob_s), zb_s, w_o_l, final_g, per_row_mod=True, final_norm=last)

        kv_t = lambda t: jnp.transpose(t.reshape(bsz, N_KV, HEAD_DIM, seq), (0, 3, 1, 2))
        kv = lambda t: t.reshape(dbsz, dseq, N_KV, HEAD_DIM)
        for lst, val in zip(outs, (kv_t(kat), kv_t(vat), jnp.transpose(kit, (0, 2, 1)), kv_t(kbt), kv_t(vbt),
                                   kv(ka_s), kv(va_s), ki_s.reshape(dbsz, dseq, IDX_DIM), kv(kb_s), kv(vb_s))):
            lst.append(val)

    return (hp, hs.reshape(dbsz, dseq, d)) + tuple(jnp.stack(o) for o in outs)
```

```python
import functools
import math

import numpy as np
import jax
import jax.numpy as jnp
from jax import lax
from jax.experimental import pallas as pl
from jax.experimental.pallas import tpu as pltpu

F32 = jnp.float32
BF16 = jnp.bfloat16
I32 = jnp.int32

HEAD_DIM = 64
N_HEADS = 8
N_KV = 4
GRP = N_HEADS // N_KV
WIDTH = N_HEADS * HEAD_DIM
KVW = N_KV * HEAD_DIM
IDX_HEADS = 8
IDX_DIM = 64
TOPK_MAX = 256
MOBA_BLOCK = 256
MOBA_TOPK = 3
REL_BUCKETS = 32
REL_MAX_DIST = 128
EPS = 1e-6
NEG = -1e30
LANES = 128
VMEM_LIMIT = 56 * 1024 * 1024

_ORIG = (("qa", WIDTH), ("ka", KVW), ("va", KVW), ("za", WIDTH), ("qi", IDX_HEADS * IDX_DIM), ("ki", IDX_DIM),
         ("wi", IDX_HEADS), ("qb", WIDTH), ("kb", KVW), ("vb", KVW), ("zb", WIDTH))
_COLS_SAMPLE = ("qa", "ka", "va", "za", "qi", "qb", "kb", "vb", "zb", "ki", "wi")
_ROWS_PROMPT_T = ("qa", "qi", "qb", "ka", "va", "kb", "vb", "ki", "wi")
_P_ZA, _P_ZB, _P_KA, _P_KEXT, _P_KI, _P_END = 0, WIDTH, 2 * WIDTH, 2 * WIDTH + KVW, 2 * WIDTH + KVW + N_KV * LANES, \
    2 * WIDTH + KVW + N_KV * LANES + LANES


def _col_ranges():
    off, o = {}, 0
    for name, sz in _ORIG:
        off[name] = (o, o + sz)
        o += sz
    return off


def _offsets(order):
    sizes = dict(_ORIG)
    off, o = {}, 0
    for name in order:
        off[name] = (o, o + sizes[name])
        o += sizes[name]
    return off, o


def _pick_cols(w, order):
    cols = _col_ranges()
    parts = [w[:, cols[n][0]:cols[n][1]] for n in order]
    pad = (-sum(p.shape[1] for p in parts)) % LANES
    if pad:
        parts.append(jnp.zeros((w.shape[0], pad), w.dtype))
    return jnp.concatenate(parts, axis=1)


def _prompt_cols(w):
    cols = _col_ranges()
    take = lambda n: w[:, cols[n][0]:cols[n][1]]
    z = jnp.zeros((w.shape[0], HEAD_DIM), w.dtype)
    kb = take("kb")
    parts = [take("za"), take("zb"), take("ka")]
    for g in range(N_KV):
        parts += [kb[:, g * HEAD_DIM:(g + 1) * HEAD_DIM], z]
    parts += [take("ki"), z]
    return jnp.concatenate(parts, axis=1)


def _bucket_np(n):
    n = np.maximum(np.asarray(n, np.int64), 0)
    exact = REL_BUCKETS // 2

    def large(dt):
        nf = np.maximum(n, 1).astype(dt)
        return exact + (np.log(nf / dt(exact)) / dt(math.log(REL_MAX_DIST / exact)) * dt(REL_BUCKETS - exact)).astype(np.int64)

    l32, l64 = large(np.float32), large(np.float64)
    assert np.array_equal(np.minimum(l32, REL_BUCKETS - 1)[n >= exact], np.minimum(l64, REL_BUCKETS - 1)[n >= exact])
    return np.where(n < exact, n, np.minimum(l32, REL_BUCKETS - 1)).astype(np.int32)


def _bucket_index(dist, valid):
    return jnp.asarray(np.where(valid, _bucket_np(dist), -1).astype(np.int32))


def _bias_table(bias, dist, valid):
    t = jnp.transpose(bias.astype(F32)[_bucket_np(dist)], (2, 0, 1))
    return jnp.where(jnp.asarray(valid)[None], t, -jnp.inf)


def _fill_bias_tables(bias_ref, idx_ref, tab_scr, b31_scr, tq):
    lead = idx_ref.shape[:-2]
    for g in range(N_KV):
        for j in range(GRP):
            head = g * GRP + j
            for ix in np.ndindex(*lead):
                idx = idx_ref[ix]

                def pick(k, t, idx=idx, head=head):
                    return jnp.where(idx == k, bias_ref[k, head], t)

                t = lax.fori_loop(0, REL_BUCKETS, pick, jnp.full(idx.shape, -jnp.inf, F32))
                tab_scr[(g,) + ix + (slice(None), slice(j * tq, (j + 1) * tq))] = t
            b31_scr[g, :, j * tq:(j + 1) * tq] = jnp.full((1, tq), bias_ref[REL_BUCKETS - 1, head], F32)


def _mod_kernel(c_ref, w_ref, b_ref, o_ref):
    c = c_ref[...]
    s = c * jax.nn.sigmoid(c)
    o_ref[...] = jnp.dot(s, w_ref[...], preferred_element_type=F32, precision=lax.Precision.HIGHEST) + b_ref[...]


def _mod_call(c, w_ada, b_ada):
    n, d = c.shape
    d3 = w_ada.shape[1]
    tn = 512 if d3 % 512 == 0 else d3
    return pl.pallas_call(
        _mod_kernel,
        grid=(d3 // tn,),
        in_specs=[pl.BlockSpec((n, d), lambda j: (0, 0)),
                  pl.BlockSpec((d, tn), lambda j: (0, j)),
                  pl.BlockSpec((1, tn), lambda j: (0, j))],
        out_specs=pl.BlockSpec((n, tn), lambda j: (0, j)),
        out_shape=jax.ShapeDtypeStruct((n, d3), F32),
        compiler_params=pltpu.CompilerParams(dimension_semantics=("arbitrary",), vmem_limit_bytes=VMEM_LIMIT),
        name="adaln_mod",
    )(c, w_ada, b_ada.reshape(1, d3))


def _proj_kernel(*refs, prompt, tm):
    if prompt:
        (x_ref, sc_ref, sh_ref, g_ref, w_ref, wt_ref,
         za_o, zb_o, kab_o, kext_o, kib_o, mean_o,
         qat_o, qit_o, qbt_o, wit_o, kat_o, vat_o, kbt_o, vbt_o, kit_o, vatb_o, vbtb_o) = refs
    else:
        (x_ref, sc_ref, sh_ref, g_ref, w_ref,
         qa_o, qi_o, qb_o, za_o, zb_o, wi_o, ka_o, va_o, kb_o, vb_o, ki_o) = refs
    x = x_ref[...]
    y = x * lax.rsqrt(jnp.mean(x * x, axis=-1, keepdims=True) + EPS) * g_ref[...]
    h = y * (1.0 + sc_ref[...]) + sh_ref[...]
    hb = h.astype(BF16)
    qscale = HEAD_DIM ** -0.5
    iscale = IDX_DIM ** -0.5
    wscale = IDX_HEADS ** -0.5

    def cols(a, b):
        return jnp.dot(hb, w_ref[:, a:b], preferred_element_type=F32)

    if prompt:
        za_o[...] = cols(_P_ZA, _P_ZB)
        zb_o[...] = cols(_P_ZB, _P_KA)
        kab_o[...] = cols(_P_KA, _P_KEXT).astype(BF16)
        kib_o[...] = cols(_P_KI, _P_END)[:, :IDX_DIM].astype(BF16)
        pos = pl.program_id(1) * tm + lax.broadcasted_iota(I32, (tm, LANES), 0)
        lane = lax.broadcasted_iota(I32, (tm, LANES), 1)
        onehot = jnp.where(lane - HEAD_DIM == jnp.right_shift(pos, int(math.log2(MOBA_BLOCK))), 1.0, 0.0)
        for g in range(N_KV):
            kx = cols(_P_KEXT + g * LANES, _P_KEXT + (g + 1) * LANES)
            kext_o[g] = (kx + onehot).astype(BF16)
            for r in range(tm // MOBA_BLOCK):
                mean_o[r:r + 1, g * LANES:(g + 1) * LANES] = jnp.mean(
                    kx[r * MOBA_BLOCK:(r + 1) * MOBA_BLOCK], axis=0, keepdims=True)
        roff, _ = _offsets(_ROWS_PROMPT_T)
        nt = (((1,), (1,)), ((), ()))

        def rows(name):
            a, b = roff[name]
            return lax.dot_general(wt_ref[a:b, :], hb, nt, preferred_element_type=F32)

        qat_o[...] = (rows("qa") * qscale).astype(BF16)
        qit_o[...] = (rows("qi") * iscale).astype(BF16)
        qbt_o[...] = (rows("qb") * qscale).astype(BF16)
        a = roff["wi"][0]
        wit_o[...] = lax.dot_general(wt_ref[a:a + 16, :], hb, nt, preferred_element_type=F32)[:IDX_HEADS] * wscale
        kat_o[...] = rows("ka")
        kbt_o[...] = rows("kb")
        kit_o[...] = rows("ki")
        vat = rows("va")
        vbt = rows("vb")
        vat_o[...] = vat
        vbt_o[...] = vbt
        vatb_o[...] = vat.astype(BF16)
        vbtb_o[...] = vbt.astype(BF16)
    else:
        off, _ = _offsets(_COLS_SAMPLE)
        seg = lambda name: cols(*off[name])
        qa_o[...] = (seg("qa") * qscale).astype(BF16)
        qi_o[...] = (seg("qi") * iscale).astype(BF16)
        qb_o[...] = (seg("qb") * qscale).astype(BF16)
        za_o[...] = seg("za")
        zb_o[...] = seg("zb")
        ka_o[...] = seg("ka")
        va_o[...] = seg("va")
        kb_o[...] = seg("kb")
        vb_o[...] = seg("vb")
        a, _ = off["ki"]
        kiwi = cols(a, a + LANES)
        ki_o[...] = kiwi[:, :IDX_DIM]
        wi_o[...] = kiwi[:, IDX_DIM:IDX_DIM + IDX_HEADS] * wscale


def _proj_call(x, scale, shift, g, w_cols, w_rows_t, *, per_row_mod):
    bx, s, d = x.shape
    tm = min(512, s)
    assert s % tm == 0 and tm % 8 == 0
    prompt = w_rows_t is not None
    nw = w_cols.shape[1]
    row = lambda c: pl.BlockSpec((None, tm, c), lambda b, i: (b, i, 0))
    mod_spec = row(d) if per_row_mod else pl.BlockSpec((None, 1, d), lambda b, i: (b, 0, 0))
    in_specs = [row(d), mod_spec, mod_spec,
                pl.BlockSpec((1, d), lambda b, i: (0, 0)),
                pl.BlockSpec((d, nw), lambda b, i: (0, 0))]
    args = [x, scale, shift, g.reshape(1, d), w_cols]
    sds = lambda c, dt: jax.ShapeDtypeStruct((bx, s, c), dt)
    if prompt:
        assert tm % MOBA_BLOCK == 0 and MOBA_BLOCK == 256 and nw == _P_END
        in_specs.append(pl.BlockSpec(w_rows_t.shape, lambda b, i: (0, 0)))
        args.append(w_rows_t)
        col_t = lambda r: pl.BlockSpec((None, r, tm), lambda b, i: (b, 0, i))
        sds_t = lambda r, dt: jax.ShapeDtypeStruct((bx, r, s), dt)
        nblk = tm // MOBA_BLOCK
        out_shape = [sds(WIDTH, F32), sds(WIDTH, F32), sds(KVW, BF16),
                     jax.ShapeDtypeStruct((bx, N_KV, s, LANES), BF16), sds(IDX_DIM, BF16),
                     jax.ShapeDtypeStruct((bx, s // tm, nblk, N_KV * LANES), F32),
                     sds_t(WIDTH, BF16), sds_t(WIDTH, BF16), sds_t(WIDTH, BF16), sds_t(IDX_HEADS, F32),
                     sds_t(KVW, F32), sds_t(KVW, F32), sds_t(KVW, F32), sds_t(KVW, F32), sds_t(IDX_DIM, F32),
                     sds_t(KVW, BF16), sds_t(KVW, BF16)]
        out_specs = [row(WIDTH), row(WIDTH), row(KVW),
                     pl.BlockSpec((None, N_KV, tm, LANES), lambda b, i: (b, 0, i, 0)), row(IDX_DIM),
                     pl.BlockSpec((None, None, nblk, N_KV * LANES), lambda b, i: (b, i, 0, 0)),
                     col_t(WIDTH), col_t(WIDTH), col_t(WIDTH), col_t(IDX_HEADS),
                     col_t(KVW), col_t(KVW), col_t(KVW), col_t(KVW), col_t(IDX_DIM),
                     col_t(KVW), col_t(KVW)]
    else:
        out_shape = [sds(WIDTH, BF16), sds(WIDTH, BF16), sds(WIDTH, BF16), sds(WIDTH, F32), sds(WIDTH, F32),
                     sds(IDX_HEADS, F32), sds(KVW, F32), sds(KVW, F32), sds(KVW, F32), sds(KVW, F32), sds(IDX_DIM, F32)]
        out_specs = [row(WIDTH)] * 5 + [row(IDX_HEADS)] + [row(KVW)] * 4 + [row(IDX_DIM)]
    return pl.pallas_call(
        functools.partial(_proj_kernel, prompt=prompt, tm=tm),
        grid=(bx, s // tm),
        in_specs=in_specs, out_specs=out_specs, out_shape=out_shape,
        compiler_params=pltpu.CompilerParams(dimension_semantics=("arbitrary", "arbitrary"),
                                             vmem_limit_bytes=VMEM_LIMIT),
        name="norm_mod_proj",
    )(*args)


ONES_ROWS = 16


def _flash_init(m_scr, acc_scr):
    m_scr[...] = jnp.full(m_scr.shape, NEG, F32)
    acc_scr[...] = jnp.zeros(acc_scr.shape, F32)


def _flash_update(sts, shifts, vts, m_scr, acc_scr):
    ps, alphas = [], []
    for g, (st, shift) in enumerate(zip(sts, shifts)):
        m_old = m_scr[g]
        mb = jnp.max(st, axis=0, keepdims=True)
        if shift is not None:
            mb = mb + shift
        m_new = jnp.maximum(m_old, mb)
        m_scr[g] = m_new
        alphas.append(jnp.exp(m_old - m_new))
        ps.append(jnp.exp((st - (m_new if shift is None else m_new - shift)).astype(BF16)))
    for g, (p, alpha, vt) in enumerate(zip(ps, alphas, vts)):
        vt1 = jnp.concatenate([vt, jnp.ones((ONES_ROWS, vt.shape[1]), BF16)], axis=0)
        acc_scr[g] = alpha * acc_scr[g] + jnp.dot(vt1, p, preferred_element_type=F32)


def _flash_finish(o_ref, acc_scr, tq):
    for g in range(N_KV):
        acc = acc_scr[g]
        o = jnp.transpose(acc[:LANES] / acc[LANES:LANES + 1])
        c0 = (g % 2) * HEAD_DIM
        for j in range(GRP):
            hd = g * GRP + j
            o_ref[:, hd * HEAD_DIM:(hd + 1) * HEAD_DIM] = o[j * tq:(j + 1) * tq, c0:c0 + HEAD_DIM]


def _f2key(x):
    b = lax.bitcast_convert_type(x, I32)
    return b ^ ((b >> 31) & jnp.int32(0x7FFFFFFF))


def _key2f(k):
    return lax.bitcast_convert_type(k ^ ((k >> 31) & jnp.int32(0x7FFFFFFF)), F32)


def _kth_threshold(count_gt, vmax, small, k, nvis):
    kf = jnp.float32(k)
    logk = jnp.float32(math.log(k))
    lo0 = jnp.full(vmax.shape, _f2key(jnp.float32(-jnp.inf)), I32)
    hi0 = jnp.where(small, lo0, _f2key(vmax))
    st0 = (jnp.int32(0), lo0, hi0, nvis.astype(F32), jnp.zeros(vmax.shape, F32), small.astype(I32))

    def cond(st):
        return jnp.logical_and(st[0] < 160, jnp.min(st[5]) == 0)

    def body(st):
        it, lo, hi, clo, chi, done = st
        mid_key = (lo & hi) + ((lo ^ hi) >> 1)
        conv = mid_key == lo
        lof = _key2f(lo)
        hif = _key2f(hi)
        llo = jnp.log(clo)
        w = (llo - logk) / (llo - jnp.log(jnp.maximum(chi, 0.5)))
        phase = lax.rem(it, 4)
        probe = jnp.where(phase == 1, 0.5 * lof + 0.5 * hif, lof + w * (hif - lof))
        pk = _f2key(probe)
        inside = jnp.logical_and(jnp.logical_and(pk > lo, pk < hi), lof > -jnp.inf)
        off_zero = jnp.logical_or(lof >= 0.0, hif < 0.0)
        use = jnp.logical_and(jnp.logical_and(inside, off_zero), phase != 3)
        mid = jnp.where(use, pk, mid_key)
        first_at_zero = jnp.logical_and(it == 0, jnp.logical_and(lo < 0, hi > 0))
        mid = jnp.where(first_at_zero, 0, mid)
        cnt = count_gt(_key2f(mid))
        active = jnp.logical_and(done == 0, jnp.logical_not(conv))
        up = jnp.logical_and(active, cnt > kf)
        down = jnp.logical_and(active, cnt <= kf)
        lo = jnp.where(up, mid, lo)
        clo = jnp.where(up, cnt, clo)
        hi = jnp.where(down, mid, hi)
        chi = jnp.where(down, cnt, chi)
        fin = jnp.logical_or(conv, jnp.logical_and(active, cnt == kf))
        return it + 1, lo, hi, clo, chi, jnp.where(fin, 1, done)

    _, _, hi, _, chi, _ = lax.while_loop(cond, body, st0)
    return _key2f(hi), chi


def _dsa_prompt_kernel(bias_ref, qit_ref, wit_ref, qat_ref, ki_ref, ka_ref, vat_ref, idx_ref, o_ref,
                       s_scr, qis_scr, qext_scr, tab_scr, b31_scr, m_scr, acc_scr, *, tq, fw, topk):
    i = pl.program_id(1)
    q0 = i * tq
    rr = GRP * tq

    @pl.when(jnp.logical_and(pl.program_id(0) == 0, i == 0))
    def _():
        _fill_bias_tables(bias_ref, idx_ref, tab_scr, b31_scr, tq)

    qit = qit_ref[...]
    qat = qat_ref[...]
    wit = wit_ref[...]
    for h in range(IDX_HEADS):
        qis_scr[:, h * tq:(h + 1) * tq] = qit[h * IDX_DIM:(h + 1) * IDX_DIM, :]
    for g in range(N_KV):
        r0 = (g % 2) * HEAD_DIM
        qext_scr[g, HEAD_DIM - r0:2 * HEAD_DIM - r0, :] = jnp.zeros((HEAD_DIM, rr), BF16)
        for j in range(GRP):
            hd = g * GRP + j
            qext_scr[g, r0:r0 + HEAD_DIM, j * tq:(j + 1) * tq] = qat[hd * HEAD_DIM:(hd + 1) * HEAD_DIM, :]

    qpos = q0 + lax.broadcasted_iota(I32, (1, tq), 1)

    nb1 = (q0 + tq + fw - 1) // fw

    def p1(jb, vmax):
        k0 = pl.multiple_of(jb * fw, fw)
        d = jnp.dot(ki_ref[pl.ds(k0, fw), :], qis_scr[...], preferred_element_type=F32)
        acc = jnp.maximum(d[:, 0:tq], 0.0) * wit[0:1, :]
        for h in range(1, IDX_HEADS):
            acc = acc + jnp.maximum(d[:, h * tq:(h + 1) * tq], 0.0) * wit[h:h + 1, :]
        key = k0 + lax.broadcasted_iota(I32, (fw, tq), 0)
        sc = jnp.where(key <= qpos, acc, -jnp.inf)
        s_scr[pl.ds(k0, fw), :] = sc
        return jnp.maximum(vmax, jnp.max(sc, axis=0, keepdims=True))

    vmax = lax.fori_loop(0, nb1 // 2, lambda m, v: p1(2 * m + 1, p1(2 * m, v)), jnp.full((1, tq), -jnp.inf, F32))
    vmax = lax.cond(nb1 % 2 == 1, lambda v: p1(nb1 - 1, v), lambda v: v, vmax)

    def count_gt(t):
        def cb(jb, c):
            x = s_scr[pl.ds(pl.multiple_of(jb * fw, fw), fw), :]
            one = jnp.where(x > t, 1.0, 0.0)
            return c + jnp.sum(one.reshape(fw // 64, 64, tq), axis=0)
        c = lax.fori_loop(0, nb1, cb, jnp.zeros((64, tq), F32))
        return jnp.sum(c, axis=0, keepdims=True)

    small = qpos < topk
    hi, chi = _kth_threshold(count_gt, vmax, small, topk, qpos + 1)
    need = jnp.where(small, 0.0, jnp.float32(topk) - chi)
    any_tie = jnp.max(need) > 0.0

    _flash_init(m_scr, acc_scr)
    nfar = jnp.maximum(q0 - LANES, 0) // fw
    near0 = nfar * (fw // LANES)

    def select(x, eqb, with_ties, width):
        gt = x > hi
        if not with_ties:
            return gt, eqb
        eq = x == hi
        e = jnp.where(eq, 1.0, 0.0)
        r = lax.broadcasted_iota(I32, (width, width), 0)
        c = lax.broadcasted_iota(I32, (width, width), 1)
        lower = jnp.where(c < r, 1.0, 0.0).astype(BF16)
        rank = jnp.dot(lower, e.astype(BF16), preferred_element_type=F32) + eqb
        sel = jnp.logical_or(gt, jnp.logical_and(eq, rank < need))
        return sel, eqb + jnp.sum(e, axis=0, keepdims=True)

    def attend(with_ties):
        def masked_qk(k0, width, eqb):
            sel, eqb = select(s_scr[pl.ds(k0, width), :], eqb, with_ties, width)
            madd = jnp.where(sel, 0.0, -jnp.inf)
            madd = jnp.concatenate([madd] * GRP, axis=1)
            sts = tuple(jnp.dot(ka_ref[pl.ds(k0, width), (g // 2) * LANES:(g // 2 + 1) * LANES], qext_scr[g],
                                preferred_element_type=F32) + madd for g in range(N_KV))
            return sts, eqb

        def softmax_pv(k0, width, sts, t):
            vts = [vat_ref[(g // 2) * LANES:(g // 2 + 1) * LANES, pl.ds(k0, width)] for g in range(N_KV)]
            if t is None:
                _flash_update(sts, [b31_scr[g] for g in range(N_KV)], vts, m_scr, acc_scr)
            else:
                _flash_update([st + tab_scr[g, t] for g, st in enumerate(sts)], [None] * N_KV, vts,
                              m_scr, acc_scr)

        def far(jb, eqb):
            k0 = pl.multiple_of(jb * fw, fw)
            sts, eqb = masked_qk(k0, fw, eqb)
            softmax_pv(k0, fw, sts, None)
            return eqb

        def near(jj, eqb):
            k0 = pl.multiple_of(jj * LANES, LANES)
            sts, eqb = masked_qk(k0, LANES, eqb)
            softmax_pv(k0, LANES, sts, jnp.clip(jj - i + 2, 0, 2))
            return eqb

        def far2(m, eqb):
            return far(2 * m + 1, far(2 * m, eqb))

        eqb = lax.fori_loop(0, nfar // 2, far2, jnp.zeros((1, tq), F32))
        eqb = lax.cond(nfar % 2 == 1, lambda e: far(nfar - 1, e), lambda e: e, eqb)
        lax.fori_loop(near0, i + 1, near, eqb)

    @pl.when(any_tie)
    def _():
        attend(True)

    @pl.when(jnp.logical_not(any_tie))
    def _():
        attend(False)

    _flash_finish(o_ref, acc_scr, tq)


def _dsa_prompt(qit, wit, qat, ki_bf, ka_bf, vat_bf, bias_a, topk):
    bx, _, s = qat.shape
    tq = LANES
    fw = min(512, s)
    rr = GRP * tq
    assert s % fw == 0 and s % tq == 0 and REL_MAX_DIST <= LANES
    c = np.arange(LANES)[:, None]
    r = np.arange(tq)[None, :]
    ones = np.ones((LANES, tq), bool)
    idx = jnp.stack([_bucket_index(np.full((LANES, tq), REL_MAX_DIST), ones),
                     _bucket_index(r - c + LANES, ones),
                     _bucket_index(r - c, (r - c) >= 0)])
    res = lambda shape: pl.BlockSpec((None,) + shape, lambda b, i: (b, 0, 0), pipeline_mode=pl.Buffered(1))
    blk_t = lambda rows: pl.BlockSpec((None, rows, tq), lambda b, i: (b, 0, i))
    return pl.pallas_call(
        functools.partial(_dsa_prompt_kernel, tq=tq, fw=fw, topk=topk),
        grid=(bx, s // tq),
        in_specs=[pl.BlockSpec(memory_space=pltpu.SMEM),
                  blk_t(IDX_HEADS * IDX_DIM), blk_t(IDX_HEADS), blk_t(WIDTH),
                  res((s, IDX_DIM)), res((s, KVW)), res((KVW, s)),
                  pl.BlockSpec(idx.shape, lambda b, i: (0, 0, 0), pipeline_mode=pl.Buffered(1))],
        out_specs=pl.BlockSpec((None, tq, WIDTH), lambda b, i: (b, i, 0)),
        out_shape=jax.ShapeDtypeStruct((bx, s, WIDTH), F32),
        scratch_shapes=[pltpu.VMEM((s, tq), F32),
                        pltpu.VMEM((IDX_DIM, IDX_HEADS * tq), BF16),
                        pltpu.VMEM((N_KV, LANES, rr), BF16),
                        pltpu.VMEM((N_KV, 3, LANES, rr), F32),
                        pltpu.VMEM((N_KV, 1, rr), F32),
                        pltpu.VMEM((N_KV, 1, rr), F32),
                        pltpu.VMEM((N_KV, LANES + ONES_ROWS, rr), F32)],
        compiler_params=pltpu.CompilerParams(dimension_semantics=("arbitrary", "arbitrary"),
                                             vmem_limit_bytes=VMEM_LIMIT),
        name="dsa_prompt",
    )(bias_a.astype(F32), qit, wit, qat, ki_bf, ka_bf, vat_bf, idx)


def _top_blocks(gate, blk, n_valid, n_sel, axis):
    nb = gate.shape[axis]
    gate = jnp.where(blk < n_valid, gate, -jnp.inf)
    sel = jnp.zeros(gate.shape, jnp.bool_)
    for _ in range(n_sel):
        mx = jnp.max(gate, axis=axis, keepdims=True)
        first = jnp.min(jnp.where(gate == mx, blk, nb), axis=axis, keepdims=True)
        pick = blk == first
        sel = jnp.logical_or(sel, pick)
        gate = jnp.where(pick, -jnp.inf, gate)
    return jnp.logical_and(sel, blk < n_valid)


def _moba_prompt_kernel(bias_ref, qbt_ref, kext_ref, vbt_ref, mean_ref, idx_ref, o_ref,
                        qext_scr, tab_scr, b31_scr, m_scr, acc_scr, *, tq, nbp, n_sel):
    i = pl.program_id(1)
    q0 = i * tq
    own = q0 // MOBA_BLOCK
    par = (q0 % MOBA_BLOCK) // tq
    rr = GRP * tq

    @pl.when(jnp.logical_and(pl.program_id(0) == 0, i == 0))
    def _():
        _fill_bias_tables(bias_ref, idx_ref, tab_scr, b31_scr, tq)

    qbt = qbt_ref[...]
    blk = lax.broadcasted_iota(I32, (nbp, rr), 0)
    zeros = jnp.zeros((HEAD_DIM, rr), BF16)
    for g in range(N_KV):
        qg = jnp.concatenate([qbt[(g * GRP + j) * HEAD_DIM:(g * GRP + j + 1) * HEAD_DIM, :] for j in range(GRP)], axis=1)
        mg = mean_ref[:, g * LANES:(g + 1) * LANES].astype(BF16)
        gate = jnp.dot(mg, jnp.concatenate([qg, zeros], axis=0), preferred_element_type=F32)
        ok = jnp.logical_or(_top_blocks(gate, blk, own, n_sel, 0), blk == own)
        qext_scr[g, 0:HEAD_DIM, :] = qg
        qext_scr[g, HEAD_DIM:HEAD_DIM + nbp, :] = jnp.where(ok, 0.0, NEG).astype(BF16)

    _flash_init(m_scr, acc_scr)

    mb = MOBA_BLOCK

    def attend(c0, width, tabs):
        sts = [jnp.dot(kext_ref[g, pl.ds(c0, width), :], qext_scr[g], preferred_element_type=F32)
               for g in range(N_KV)]
        vts = [vbt_ref[(g // 2) * LANES:(g // 2 + 1) * LANES, pl.ds(c0, width)] for g in range(N_KV)]
        if tabs is None:
            _flash_update(sts, [b31_scr[g] for g in range(N_KV)], vts, m_scr, acc_scr)
        else:
            _flash_update([st + t for st, t in zip(sts, tabs)], [None] * N_KV, vts, m_scr, acc_scr)

    nfar = jnp.maximum(own - 1, 0)

    def far4(m, carry):
        c0 = pl.multiple_of(m * 4 * mb, 4 * mb)
        attend(c0, 2 * mb, None)
        attend(c0 + 2 * mb, 2 * mb, None)
        return carry

    lax.fori_loop(0, nfar // 4, far4, 0)

    @pl.when((nfar // 2) % 2 == 1)
    def _():
        attend(pl.multiple_of((nfar // 4) * 4 * mb, 2 * mb), 2 * mb, None)

    @pl.when(nfar % 2 == 1)
    def _():
        attend(pl.multiple_of((nfar - 1) * mb, mb), mb, None)

    @pl.when(own >= 1)
    def _():
        attend(pl.multiple_of((own - 1) * mb, mb), 2 * mb, [tab_scr[g, par] for g in range(N_KV)])

    @pl.when(own == 0)
    def _():
        attend(0, mb, [tab_scr[g, par, mb:2 * mb] for g in range(N_KV)])

    _flash_finish(o_ref, acc_scr, tq)


def _moba_prompt(qbt, kext, vbt_bf, means, bias_b):
    bx, _, s = qbt.shape
    tq = LANES
    nb = s // MOBA_BLOCK
    nbp = HEAD_DIM
    rr = GRP * tq
    npar = MOBA_BLOCK // tq
    assert s % MOBA_BLOCK == 0 and nb <= nbp and MOBA_BLOCK % tq == 0 and REL_MAX_DIST <= MOBA_BLOCK
    n_sel = min(MOBA_TOPK, nb - 1)
    means = jnp.pad(means, ((0, 0), (0, nbp - nb), (0, 0)))
    c = np.arange(MOBA_BLOCK)[:, None]
    r = np.arange(tq)[None, :]
    slabs = []
    for par in range(npar):
        d_own = par * tq + r - c
        slabs.append(jnp.concatenate([_bucket_index(d_own + MOBA_BLOCK, np.ones_like(d_own, bool)),
                                      _bucket_index(d_own, d_own >= 0)], axis=0))
    idx = jnp.stack(slabs)
    res = lambda shape: pl.BlockSpec((None,) + shape, lambda b, i: (b,) + (0,) * len(shape),
                                     pipeline_mode=pl.Buffered(1))
    return pl.pallas_call(
        functools.partial(_moba_prompt_kernel, tq=tq, nbp=nbp, n_sel=n_sel),
        grid=(bx, s // tq),
        in_specs=[pl.BlockSpec(memory_space=pltpu.SMEM),
                  pl.BlockSpec((None, WIDTH, tq), lambda b, i: (b, 0, i)),
                  res((N_KV, s, LANES)), res((KVW, s)), res((nbp, N_KV * LANES)),
                  pl.BlockSpec(idx.shape, lambda b, i: (0, 0, 0), pipeline_mode=pl.Buffered(1))],
        out_specs=pl.BlockSpec((None, tq, WIDTH), lambda b, i: (b, i, 0)),
        out_shape=jax.ShapeDtypeStruct((bx, s, WIDTH), F32),
        scratch_shapes=[pltpu.VMEM((N_KV, HEAD_DIM + nbp, rr), BF16),
                        pltpu.VMEM((N_KV, npar, 2 * MOBA_BLOCK, rr), F32),
                        pltpu.VMEM((N_KV, 1, rr), F32),
                        pltpu.VMEM((N_KV, 1, rr), F32),
                        pltpu.VMEM((N_KV, LANES + ONES_ROWS, rr), F32)],
        compiler_params=pltpu.CompilerParams(dimension_semantics=("arbitrary", "arbitrary"),
                                             vmem_limit_bytes=VMEM_LIMIT),
        name="moba_prompt",
    )(bias_b.astype(F32), qbt, kext, vbt_bf, means, idx)


def _sample_kernel(*refs, mode, past, page, ds, ck, topk, n_sel):
    if mode == "dsa":
        (pt_ref, qbd_ref, kf_ref, vf_ref, btab_ref, b31_ref, qis_ref, wic_ref, kif_ref,
         ipool, kpool, vpool, o_ref, kbuf, vbuf, ibuf, sem, lg_scr, madd_scr, sc_scr) = refs
        pools, bufs = (kpool, vpool, ipool), (kbuf, vbuf, ibuf)
    else:
        (pt_ref, qbd_ref, kf_ref, vf_ref, btab_ref, b31_ref,
         kpool, vpool, o_ref, kbuf, vbuf, sem, lg_scr, madd_scr, mean_scr) = refs
        pools, bufs = (kpool, vpool), (kbuf, vbuf)
    b = pl.program_id(0)
    nseq = pl.num_programs(0)
    slot = b % 2
    npages = past // page
    nck = past // ck
    lk = past + LANES
    rows = N_HEADS * ds
    nt = (((1,), (1,)), ((), ()))

    def copies(seq, sl):
        out = []
        for a, (pool, buf) in enumerate(zip(pools, bufs)):
            def mk(pg, pool=pool, buf=buf, a=a):
                return pltpu.make_async_copy(pool.at[pt_ref[seq, pg]],
                                             buf.at[sl, :, pl.ds(pl.multiple_of(pg * page, page), page)],
                                             sem.at[a, sl])
            out.append(mk)
        return out

    def start_all(seq, sl):
        def body(pg, c):
            for mk in copies(seq, sl):
                mk(pg).start()
            return c
        lax.fori_loop(0, npages, body, 0)

    def wait_all(seq, sl):
        def body(pg, c):
            for mk in copies(seq, sl):
                mk(pg).wait()
            return c
        lax.fori_loop(0, npages, body, 0)

    @pl.when(b == 0)
    def _():
        start_all(b, slot)

    @pl.when(b + 1 < nseq)
    def _():
        start_all(b + 1, 1 - slot)

    wait_all(b, slot)

    def tail(ref):
        return jnp.concatenate([ref[...], jnp.zeros((LANES - ds, ref.shape[1]), F32)], axis=0).astype(BF16)

    kf_tail = tail(kf_ref)
    vf_tail = tail(vf_ref)
    qrow = lax.broadcasted_iota(I32, (ds, 1), 0)

    if mode == "dsa":
        qis = qis_ref[...]
        wic = wic_ref[...]

        def head_sum(d):
            t = jnp.maximum(d, 0.0) * wic
            return jnp.sum(t.reshape(IDX_HEADS, ds, d.shape[1]), axis=0)

        def sc_body(c, carry):
            r0 = pl.multiple_of(c * ck, ck)
            d = jnp.dot(qis, ibuf[slot, :, pl.ds(r0, ck)].astype(BF16), preferred_element_type=F32)
            sc_scr[:, pl.ds(r0, ck)] = head_sum(d)
            return carry

        lax.fori_loop(0, nck, sc_body, 0)
        lane = lax.broadcasted_iota(I32, (ds, LANES), 1)
        d_tail = lax.dot_general(qis, tail(kif_ref), nt, preferred_element_type=F32)
        sc_scr[:, past:lk] = jnp.where(lane <= qrow, head_sum(d_tail), -jnp.inf)

        x = sc_scr[...]
        rowmax = jnp.max(x, axis=1, keepdims=True)
        small = (past + qrow) < topk

        def count_gt(t):
            return jnp.sum(jnp.where(sc_scr[...] > t, 1.0, 0.0), axis=1, keepdims=True)

        hi, chi = _kth_threshold(count_gt, rowmax, small, topk, past + qrow + 1)
        need = jnp.where(small, 0.0, jnp.float32(topk) - chi)
        any_tie = jnp.max(need) > 0.0

        @pl.when(jnp.logical_not(any_tie))
        def _():
            madd_scr[...] = jnp.where(sc_scr[...] > hi, 0.0, -jnp.inf)

        @pl.when(any_tie)
        def _():
            r = lax.broadcasted_iota(I32, (LANES, LANES), 0)
            c = lax.broadcasted_iota(I32, (LANES, LANES), 1)
            upper = jnp.where(r < c, 1.0, 0.0).astype(BF16)

            def tb(cix, eqb):
                c0 = pl.multiple_of(cix * LANES, LANES)
                xx = sc_scr[:, pl.ds(c0, LANES)]
                eq = xx == hi
                e = jnp.where(eq, 1.0, 0.0)
                rank = jnp.dot(e.astype(BF16), upper, preferred_element_type=F32) + eqb
                sel = jnp.logical_or(xx > hi, jnp.logical_and(eq, rank < need))
                madd_scr[:, pl.ds(c0, LANES)] = jnp.where(sel, 0.0, -jnp.inf)
                return eqb + jnp.sum(e, axis=1, keepdims=True)

            lax.fori_loop(0, lk // LANES, tb, jnp.zeros((ds, 1), F32))

        def madd_at(r0, width):
            m = madd_scr[:, pl.ds(r0, width)]
            return jnp.broadcast_to(m[None], (N_HEADS, ds, width)).reshape(rows, width)
    else:
        nbs = past // MOBA_BLOCK
        for n in range(nbs):
            mean_scr[:, n:n + 1] = jnp.mean(kbuf[slot, :, n * MOBA_BLOCK:(n + 1) * MOBA_BLOCK], axis=1, keepdims=True)
        gate = jnp.dot(qbd_ref[...], mean_scr[...].astype(BF16), preferred_element_type=F32)
        blk = lax.broadcasted_iota(I32, (rows, nbs), 1)
        ok = _top_blocks(gate, blk, nbs, n_sel, 1)
        mv = jnp.where(ok, 0.0, -jnp.inf)
        for n in range(nbs):
            madd_scr[:, n * MOBA_BLOCK:(n + 1) * MOBA_BLOCK] = jnp.broadcast_to(mv[:, n:n + 1], (rows, MOBA_BLOCK))
        madd_scr[:, past:lk] = jnp.zeros((rows, LANES), F32)

        def madd_at(r0, width):
            return madd_scr[:, pl.ds(r0, width)]

    qbd = qbd_ref[...]
    b31 = b31_ref[...]

    def lg_body(c, m):
        r0 = pl.multiple_of(c * ck, ck)
        lg = jnp.dot(qbd, kbuf[slot, :, pl.ds(r0, ck)].astype(BF16), preferred_element_type=F32)
        lg = lg + madd_at(r0, ck) + jnp.where(c == nck - 1, btab_ref[:, 0:ck], b31)
        lg_scr[:, pl.ds(r0, ck)] = lg
        return jnp.maximum(m, jnp.max(lg, axis=1, keepdims=True))

    m = lax.fori_loop(0, nck, lg_body, jnp.full((rows, 1), NEG, F32))
    lgt = (lax.dot_general(qbd, kf_tail, nt, preferred_element_type=F32)
           + madd_at(past, LANES) + btab_ref[:, ck:ck + LANES])
    m = jnp.maximum(m, jnp.max(lgt, axis=1, keepdims=True))

    def pv_body(c, carry):
        l, acc = carry
        r0 = pl.multiple_of(c * ck, ck)
        p = jnp.exp(lg_scr[:, pl.ds(r0, ck)] - m)
        vc = vbuf[slot, :, pl.ds(r0, ck)].astype(BF16)
        return (l + jnp.sum(p, axis=1, keepdims=True),
                acc + lax.dot_general(p.astype(BF16), vc, nt, preferred_element_type=F32))

    l, acc = lax.fori_loop(0, nck, pv_body, (jnp.zeros((rows, 1), F32), jnp.zeros((rows, KVW), F32)))
    pt = jnp.exp(lgt - m)
    l = l + jnp.sum(pt, axis=1, keepdims=True)
    acc = acc + jnp.dot(pt.astype(BF16), vf_tail, preferred_element_type=F32)
    o_ref[...] = acc / l


def _sample_attn(mode, page_table, q_s, k_fresh, v_fresh, k_pool, v_pool, bias, *, topk=0, n_sel=0,
                 qi_s=None, wi_s=None, ki_fresh=None, i_pool=None):
    db, ds, _ = q_s.shape
    npages = page_table.shape[1]
    page = k_pool.shape[2]
    past = npages * page
    ck = min(1024, past)
    rows = N_HEADS * ds
    lk = past + LANES
    assert past % ck == 0 and ck >= REL_MAX_DIST + ds and ds <= LANES and past % MOBA_BLOCK == 0 and ds % 8 == 0

    qh = jnp.transpose(q_s.reshape(db, ds, N_HEADS, HEAD_DIM), (0, 2, 1, 3))
    onehot = jnp.asarray((np.arange(N_HEADS)[:, None] // GRP == np.arange(N_KV)[None, :]).astype(np.float32), BF16)
    qbd = (qh[:, :, :, None, :] * onehot[None, :, None, :, None]).reshape(db, rows, KVW)

    hq = np.arange(ds)[:, None]
    c = np.arange(ck + LANES)[None, :]
    dist = np.where(c < ck, ck + hq - c, hq - (c - ck))
    valid = np.where(c < ck, True, (c - ck) <= hq)
    btab = _bias_table(bias, dist, valid).reshape(rows, ck + LANES)
    b31 = jnp.repeat(bias[REL_BUCKETS - 1].astype(F32), ds).reshape(rows, 1)

    per = lambda shape: pl.BlockSpec((None,) + shape, lambda b, pt: (b, 0, 0))
    const = lambda a: pl.BlockSpec(a.shape, lambda b, pt: (0,) * a.ndim)
    anyspec = pl.BlockSpec(memory_space=pl.ANY)
    in_specs = [per((rows, KVW)), per((ds, KVW)), per((ds, KVW)), const(btab), const(b31)]
    args = [qbd, k_fresh, v_fresh, btab, b31]
    scratch = [pltpu.VMEM((2, KVW, past), F32), pltpu.VMEM((2, KVW, past), F32)]
    if mode == "dsa":
        qis = jnp.transpose(qi_s.reshape(db, ds, IDX_HEADS, IDX_DIM), (0, 2, 1, 3)).reshape(db, IDX_HEADS * ds, IDX_DIM)
        wic = jnp.transpose(wi_s, (0, 2, 1)).reshape(db, IDX_HEADS * ds, 1)
        in_specs += [per((IDX_HEADS * ds, IDX_DIM)), per((IDX_HEADS * ds, 1)), per((ds, IDX_DIM)), anyspec]
        args += [qis, wic, ki_fresh, i_pool]
        scratch += [pltpu.VMEM((2, IDX_DIM, past), F32), pltpu.SemaphoreType.DMA((3, 2)),
                    pltpu.VMEM((rows, lk), F32), pltpu.VMEM((ds, lk), F32), pltpu.VMEM((ds, lk), F32)]
    else:
        scratch += [pltpu.SemaphoreType.DMA((2, 2)),
                    pltpu.VMEM((rows, lk), F32), pltpu.VMEM((rows, lk), F32),
                    pltpu.VMEM((KVW, past // MOBA_BLOCK), F32)]
    in_specs += [anyspec, anyspec]
    args += [k_pool, v_pool]
    o = pl.pallas_call(
        functools.partial(_sample_kernel, mode=mode, past=past, page=page, ds=ds, ck=ck, topk=topk, n_sel=n_sel),
        grid_spec=pltpu.PrefetchScalarGridSpec(
            num_scalar_prefetch=1, grid=(db,), in_specs=in_specs,
            out_specs=pl.BlockSpec((None, rows, KVW), lambda b, pt: (b, 0, 0)),
            scratch_shapes=scratch),
        out_shape=jax.ShapeDtypeStruct((db, rows, KVW), F32),
        compiler_params=pltpu.CompilerParams(dimension_semantics=("arbitrary",), vmem_limit_bytes=VMEM_LIMIT),
        name="sample_" + mode,
    )(page_table, *args)
    o5 = o.reshape(db, N_HEADS, ds, N_KV, HEAD_DIM)
    og = o5[:, np.arange(N_HEADS), :, np.arange(N_HEADS) // GRP, :]
    return jnp.transpose(og, (1, 2, 0, 3)).reshape(db, ds, WIDTH)


def _out_kernel(x_ref, gate_ref, oa_ref, za_ref, ob_ref, zb_ref, wo_ref, fg_ref, y_ref, *, final_norm):
    za = za_ref[...]
    zb = zb_ref[...]
    ma = (oa_ref[...] * (za * jax.nn.sigmoid(za))).astype(BF16)
    mb = (ob_ref[...] * (zb * jax.nn.sigmoid(zb))).astype(BF16)
    r = (jnp.dot(ma, wo_ref[0:WIDTH, :], preferred_element_type=F32)
         + jnp.dot(mb, wo_ref[WIDTH:2 * WIDTH, :], preferred_element_type=F32))
    h = x_ref[...] + gate_ref[...] * r
    if final_norm:
        h = h * lax.rsqrt(jnp.mean(h * h, axis=-1, keepdims=True) + EPS) * fg_ref[...]
    y_ref[...] = h


def _out_call(x, gate, oa, za, ob, zb, w_o, final_g, *, per_row_mod, final_norm):
    bx, s, d = x.shape
    tm = min(512, s)
    assert s % tm == 0
    row = lambda c: pl.BlockSpec((None, tm, c), lambda b, i: (b, i, 0))
    gate_spec = row(d) if per_row_mod else pl.BlockSpec((None, 1, d), lambda b, i: (b, 0, 0))
    return pl.pallas_call(
        functools.partial(_out_kernel, final_norm=final_norm),
        grid=(bx, s // tm),
        in_specs=[row(d), gate_spec, row(WIDTH), row(WIDTH), row(WIDTH), row(WIDTH),
                  pl.BlockSpec(w_o.shape, lambda b, i: (0, 0)), pl.BlockSpec((1, d), lambda b, i: (0, 0))],
        out_specs=row(d),
        out_shape=jax.ShapeDtypeStruct((bx, s, d), F32),
        compiler_params=pltpu.CompilerParams(dimension_semantics=("arbitrary", "arbitrary"),
                                             vmem_limit_bytes=VMEM_LIMIT),
        name="gate_out_residual",
    )(x, gate, oa, za, ob, zb, w_o, final_g.reshape(1, d))


def kernel(x_prompt, x_sample, c_prompt, c_sample, cache_k_a, cache_v_a, cache_idx_k, cache_k_b, cache_v_b,
           page_table, rel_bias, norm_g, w_ada, b_ada, w_in, w_o, final_g):
    bsz, seq, d = x_prompt.shape
    dbsz, dseq, _ = x_sample.shape
    depth = w_in.shape[0]
    n_pool, page = cache_k_a.shape[1], cache_k_a.shape[2]
    past = page_table.shape[1] * page
    bias_a = rel_bias[:, :N_HEADS]
    bias_b = rel_bias[:, N_HEADS:]
    topk_p = min(TOPK_MAX, seq // 4)
    topk_s = min(TOPK_MAX, (past + dseq) // 4)
    nsel_s = min(MOBA_TOPK, past // MOBA_BLOCK)

    hp = x_prompt
    hs = x_sample.reshape(1, dbsz * dseq, d)
    outs = [[] for _ in range(10)]
    for l in range(depth):
        w_l = w_in[l]
        w_cols_p = _prompt_cols(w_l).astype(BF16)
        w_rows_t = jnp.transpose(_pick_cols(w_l, _ROWS_PROMPT_T)).astype(BF16)
        w_cols_s = _pick_cols(w_l, _COLS_SAMPLE).astype(BF16)
        w_o_l = w_o[l].astype(BF16)
        last = l == depth - 1

        mod = _mod_call(jnp.concatenate([c_prompt, c_sample], axis=0), w_ada[l], b_ada[l])
        shift, scale, gate = jnp.split(mod, 3, axis=-1)

        pm = lambda t: t[:bsz].reshape(bsz, 1, d)
        (za, zb, ka_bf, kext, ki_bf, means, qat, qit, qbt, wit, kat, vat, kbt, vbt, kit, vat_bf, vbt_bf) = _proj_call(
            hp, pm(scale), pm(shift), norm_g[l], w_cols_p, w_rows_t, per_row_mod=False)
        oa = _dsa_prompt(qit, wit, qat, ki_bf, ka_bf, vat_bf, bias_a, topk_p)
        ob = _moba_prompt(qbt, kext, vbt_bf, means.reshape(bsz, seq // MOBA_BLOCK, N_KV * LANES), bias_b)
        hp = _out_call(hp, pm(gate), oa, za, ob, zb, w_o_l, final_g, per_row_mod=False, final_norm=last)

        sm = lambda t: jnp.broadcast_to(t[bsz:, None, :], (dbsz, dseq, d)).reshape(1, dbsz * dseq, d)
        (qa_s, qi_s, qb_s, za_s, zb_s, wi_s, ka_s, va_s, kb_s, vb_s, ki_s) = _proj_call(
            hs, sm(scale), sm(shift), norm_g[l], w_cols_s, None, per_row_mod=True)
        sq = lambda t: t.reshape(dbsz, dseq, t.shape[-1])
        pool = lambda c: jnp.moveaxis(c[l], 1, -1).reshape(n_pool, -1, page)
        oa_s = _sample_attn("dsa", page_table, sq(qa_s), sq(ka_s), sq(va_s), pool(cache_k_a), pool(cache_v_a), bias_a,
                            topk=topk_s, qi_s=sq(qi_s), wi_s=sq(wi_s), ki_fresh=sq(ki_s), i_pool=pool(cache_idx_k))
        ob_s = _sample_attn("moba", page_table, sq(qb_s), sq(kb_s), sq(vb_s), pool(cache_k_b), pool(cache_v_b), bias_b,
                            n_sel=nsel_s)
        flat = lambda t: t.reshape(1, dbsz * dseq, WIDTH)
        hs = _out_call(hs, sm(gate), flat(oa_s), za_s, flat(ob_s), zb_s, w_o_l, final_g, per_row_mod=True, final_norm=last)

        kv_t = lambda t: jnp.transpose(t.reshape(bsz, N_KV, HEAD_DIM, seq), (0, 3, 1, 2))
        kv = lambda t: t.reshape(dbsz, dseq, N_KV, HEAD_DIM)
        for lst, val in zip(outs, (kv_t(kat), kv_t(vat), jnp.transpose(kit, (0, 2, 1)), kv_t(kbt), kv_t(vbt),
                                   kv(ka_s), kv(va_s), ki_s.reshape(dbsz, dseq, IDX_DIM), kv(kb_s), kv(vb_s))):
            lst.append(val)

    return (hp, hs.reshape(dbsz, dseq, d)) + tuple(jnp.stack(o) for o in outs)
```
